```python
import math
import jax
import jax.numpy as jnp
from jax import lax
import numpy as np

D_MODEL = 4096
BATCH = 1
SEQ = 16384
DEPTH = 1

HEAD_DIM = 128
MIX_WIDTH = D_MODEL
N_HEADS_A = MIX_WIDTH // (2 * HEAD_DIM)
N_HEADS_B = MIX_WIDTH // (2 * HEAD_DIM)
N_HEADS = N_HEADS_A + N_HEADS_B
DILATED_PATTERNS = ((128, 1), (512, 4), (2048, 16))
N_KV_B = 4
Q_PER_KV_B = N_HEADS_B // N_KV_B
N_BRANCH_B = 3
CMP_BLOCK = 32
CMP_STRIDE = 16
CMP_HIDDEN = HEAD_DIM
SLC_BLOCK = 64
N_SLC = 16
WIN_B = 512
Q_BLOCK = 128
N_BUCKETS = 32
MAX_DISTANCE = 2048
PEER_HEADS = 8
N_KEYS = 128
N_EXPERTS = N_KEYS * N_KEYS
PEER_TOPK = 16
PEER_DQ = 256
TOKEN_BLOCK = 128
RMS_EPS = 1e-6
NEG = -1e30

IN_COLS = 3 * N_HEADS_A * HEAD_DIM + N_HEADS_B * HEAD_DIM + 2 * N_BRANCH_B * N_KV_B * HEAD_DIM + N_BRANCH_B * N_HEADS_B

kernel_name = 'hybrid_dilated_nsa_peer_block'


def rms_norm(x, g):
    xf = x.astype(jnp.float32)
    y = xf * lax.rsqrt(jnp.mean(xf * xf, axis=-1, keepdims=True) + RMS_EPS)
    return (y * g.astype(jnp.float32)).astype(x.dtype)


def rel_bucket(dist):
    n = jnp.maximum(dist, 0)
    max_exact = N_BUCKETS // 2
    nf = jnp.maximum(n, 1).astype(jnp.float32)
    log_part = jnp.log(nf / max_exact) / math.log(MAX_DISTANCE / max_exact) * (N_BUCKETS - max_exact)
    large = jnp.minimum(max_exact + log_part.astype(jnp.int32), N_BUCKETS - 1)
    return jnp.where(n < max_exact, n, large)


def banded_attention(q, k, v, max_dist, dist_scale, bias_t):
    b, xx, hk, r, length, hd = q.shape
    nb = -(-length // Q_BLOCK)
    lp = nb * Q_BLOCK
    n_prev = -(-max_dist // Q_BLOCK)
    front = n_prev * Q_BLOCK
    span = front + Q_BLOCK
    qp = jnp.pad(q, ((0, 0),) * 4 + ((0, lp - length), (0, 0)))
    kp = jnp.pad(k, ((0, 0),) * 3 + ((front, lp - length), (0, 0)))
    vp = jnp.pad(v, ((0, 0),) * 3 + ((front, lp - length), (0, 0)))
    q_loc = jnp.arange(Q_BLOCK)[:, None]
    k_loc = jnp.arange(span)[None, :] - front
    rel = q_loc - k_loc
    band = (rel >= 0) & (rel <= max_dist)
    bias = jnp.take(bias_t.astype(jnp.float32), rel_bucket(rel * dist_scale), axis=1).reshape(hk, r, Q_BLOCK, span)
    scale = HEAD_DIM ** -0.5

    def one_block(n):
        start = n * Q_BLOCK
        qb = lax.dynamic_slice_in_dim(qp, start, Q_BLOCK, axis=4)
        kb = lax.dynamic_slice_in_dim(kp, start, span, axis=3)
        vb = lax.dynamic_slice_in_dim(vp, start, span, axis=3)
        s = jnp.einsum('bxhrqd,bxhkd->bxhrqk', qb, kb).astype(jnp.float32) * scale + bias
        s = jnp.where(band & (start + k_loc >= 0), s, NEG)
        m = jnp.max(s, axis=-1, keepdims=True)
        p = jnp.exp(s - m)
        den = jnp.sum(p, axis=-1)
        o = jnp.einsum('bxhrqk,bxhkd->bxhrqd', p.astype(vb.dtype), vb) / den[..., None]
        return o.astype(q.dtype), m[..., 0] + jnp.log(den)

    o, lse = lax.map(one_block, jnp.arange(nb))
    o = jnp.moveaxis(o, 0, 4).reshape(b, xx, hk, r, lp, hd)[:, :, :, :, :length]
    lse = jnp.moveaxis(lse, 0, 4).reshape(b, xx, hk, r, lp)[..., :length]
    return o, lse


def dilated_attention(q, k, v, bias_t):
    b, s, h, hd = q.shape
    outs, lses = [], []
    for window, dil in DILATED_PATTERNS:
        length = s // dil

        def to_sub(t):
            return t.reshape(b, length, dil, h, hd).transpose(0, 2, 3, 1, 4)

        o, lse = banded_attention(to_sub(q)[:, :, :, None], to_sub(k), to_sub(v), window // dil, dil, bias_t)
        outs.append(o[:, :, :, 0].transpose(0, 3, 1, 2, 4).reshape(b, s, h, hd))
        lses.append(lse[:, :, :, 0].transpose(0, 3, 1, 2).reshape(b, s, h))
    wts = jax.nn.softmax(jnp.stack(lses, axis=-1), axis=-1)
    o = jnp.einsum('bshpd,bshp->bshd', jnp.stack(outs, axis=3).astype(jnp.float32), wts)
    return o.astype(q.dtype)


def nsa_attention(q, k_cmp, v_cmp, k_slc, v_slc, k_win, v_win, gate_logits,
                  k_cmp_gain, cmp_pos, cmp_w1, cmp_w2, bias_t):
    b, s, _, hd = q.shape
    g_, r_ = N_KV_B, Q_PER_KV_B
    scale = hd ** -0.5
    bias_gr = bias_t.astype(jnp.float32).reshape(g_, r_, N_BUCKETS)

    n_cmp = (s - CMP_BLOCK) // CMP_STRIDE + 1
    blk_idx = jnp.arange(n_cmp)[:, None] * CMP_STRIDE + jnp.arange(CMP_BLOCK)[None, :]

    def compress(t, pos, w1, w2):
        tb = t[:, blk_idx] + pos[None, None, :, None, :]
        tb = tb.transpose(0, 1, 3, 2, 4).reshape(b, n_cmp, g_, CMP_BLOCK * hd)
        return jax.nn.gelu(tb @ w1) @ w2

    kc = rms_norm(compress(k_cmp, cmp_pos[0], cmp_w1[0], cmp_w2[0]), k_cmp_gain)
    vc = compress(v_cmp, cmp_pos[1], cmp_w1[1], cmp_w2[1])
    cmp_end = jnp.arange(n_cmp) * CMP_STRIDE + CMP_BLOCK - 1

    n_slc = s // SLC_BLOCK
    k_sel = min(N_SLC, n_slc)
    ratio = SLC_BLOCK // CMP_STRIDE
    left = CMP_BLOCK // CMP_STRIDE - 1
    right = max(ratio * n_slc - n_cmp, 0)
    taps = [(o + left, (min(o * CMP_STRIDE + CMP_BLOCK, SLC_BLOCK) - max(o * CMP_STRIDE, 0)) / CMP_STRIDE)
            for o in range(-left, ratio)]

    kb_slc = k_slc.transpose(0, 2, 1, 3).reshape(b, g_, n_slc, SLC_BLOCK, hd)
    vb_slc = v_slc.transpose(0, 2, 1, 3).reshape(b, g_, n_slc, SLC_BLOCK, hd)
    gather_blocks = jax.vmap(jax.vmap(lambda blocks, idx: blocks[idx]))
    qg = q.reshape(b, s, g_, r_, hd)
    j_idx = jnp.arange(n_slc)
    g_ix = jnp.arange(g_)[None, :, None, None, None, None]
    r_ix = jnp.arange(r_)[None, None, :, None, None, None]

    def query_block(n):
        start = n * Q_BLOCK
        qb = lax.dynamic_slice_in_dim(qg, start, Q_BLOCK, axis=1)
        t = start + jnp.arange(Q_BLOCK)
        sc = jnp.einsum('bqgrd,bigd->bgrqi', qb, kc).astype(jnp.float32) * scale
        rel = t[:, None] - cmp_end[None, :]
        vis = rel >= 0
        sc = jnp.where(vis, sc + jnp.take(bias_gr, rel_bucket(rel), axis=2), NEG)
        p = jnp.where(vis, jnp.exp(sc - jnp.max(sc, axis=-1, keepdims=True)), 0.0)
        p = p / jnp.maximum(jnp.sum(p, axis=-1, keepdims=True), 1e-30)
        o_cmp = jnp.einsum('bgrqi,bigd->bqgrd', p.astype(vc.dtype), vc)
        imp = jnp.pad(jnp.sum(p, axis=2), ((0, 0), (0, 0), (0, 0), (left, right)))
        imp = sum(w * imp[..., a:a + ratio * (n_slc - 1) + 1:ratio] for a, w in taps)
        cur = t // SLC_BLOCK
        forced = (j_idx[None, :] == 0) | (j_idx[None, :] == cur[:, None]) | (j_idx[None, :] == cur[:, None] - 1)
        allowed = j_idx[None, :] <= cur[:, None]
        score = jnp.where(forced, 1e9, jnp.where(allowed, imp, NEG))
        _, sel = lax.top_k(score, k_sel)
        sel_ok = sel <= cur[None, None, :, None]
        ks = gather_blocks(kb_slc, sel)
        vs = gather_blocks(vb_slc, sel)
        kpos = sel[..., None] * SLC_BLOCK + jnp.arange(SLC_BLOCK)
        rel_s = t[None, None, :, None, None] - kpos
        vis_s = (rel_s >= 0) & sel_ok[..., None]
        s2 = jnp.einsum('bqgrd,bgqkcd->bgrqkc', qb, ks).astype(jnp.float32) * scale
        s2 = s2 + bias_gr[g_ix, r_ix, rel_bucket(rel_s)[:, :, None]]
        s2 = jnp.where(vis_s[:, :, None], s2, NEG).reshape(b, g_, r_, Q_BLOCK, -1)
        p2 = jax.nn.softmax(s2, axis=-1)
        o_slc = jnp.einsum('bgrqn,bgqnd->bqgrd', p2.astype(vs.dtype), vs.reshape(b, g_, Q_BLOCK, -1, hd))
        return o_cmp.astype(q.dtype), o_slc.astype(q.dtype)

    o_cmp, o_slc = lax.map(query_block, jnp.arange(s // Q_BLOCK))
    o_cmp = jnp.moveaxis(o_cmp, 0, 1).reshape(b, s, N_HEADS_B, hd)
    o_slc = jnp.moveaxis(o_slc, 0, 1).reshape(b, s, N_HEADS_B, hd)

    qw = qg.transpose(0, 2, 3, 1, 4)[:, None]
    o_win, _ = banded_attention(qw, k_win.transpose(0, 2, 1, 3)[:, None], v_win.transpose(0, 2, 1, 3)[:, None],
                                WIN_B - 1, 1, bias_t)
    o_win = o_win[:, 0].transpose(0, 3, 1, 2, 4).reshape(b, s, N_HEADS_B, hd)

    gts = jax.nn.sigmoid(gate_logits.astype(jnp.float32))
    out = gts[..., 0:1] * o_cmp + gts[..., 1:2] * o_slc + gts[..., 2:3] * o_win
    return out.astype(q.dtype)


def peer_ffn(xn, w_query, sub_keys, down, up):
    b, s, d = xn.shape
    q = (xn @ w_query).reshape(b, s, PEER_HEADS, 2, PEER_DQ // 2)
    sc = jnp.einsum('bshpd,hpkd->bshpk', q, sub_keys).astype(jnp.float32)
    s1, i1 = lax.top_k(sc[:, :, :, 0], PEER_TOPK)
    s2, i2 = lax.top_k(sc[:, :, :, 1], PEER_TOPK)
    n_cand = PEER_TOPK * PEER_TOPK
    cand_s = (s1[..., :, None] + s2[..., None, :]).reshape(b, s, PEER_HEADS, n_cand)
    cand_i = (i1[..., :, None] * N_KEYS + i2[..., None, :]).reshape(b, s, PEER_HEADS, n_cand)
    top_s, pos = lax.top_k(cand_s, PEER_TOPK)
    experts = jnp.take_along_axis(cand_i, pos, axis=-1)
    gates = jax.nn.softmax(top_s, axis=-1)
    n_blk = (b * s) // TOKEN_BLOCK
    ne = PEER_HEADS * PEER_TOPK
    xb = xn.reshape(n_blk, TOKEN_BLOCK, d)
    eb = experts.reshape(n_blk, TOKEN_BLOCK, ne)
    gb = gates.reshape(n_blk, TOKEN_BLOCK, ne)

    def one_block(args):
        xt, et, gt = args
        a = jax.nn.gelu(jnp.einsum('td,ted->te', xt, down[et]).astype(jnp.float32))
        return jnp.einsum('te,ted->td', (gt * a).astype(xt.dtype), up[et])

    return lax.map(one_block, (xb, eb, gb)).reshape(b, s, d)


def setup_inputs(seed: int = 0) -> dict:
    key = jax.random.key(seed)
    ks = jax.random.split(key, 16)
    f32 = jnp.float32

    def nrm(k, shape, scale):
        return jax.random.normal(k, shape, f32) * scale

    def gain(k, shape):
        return 1.0 + 0.02 * jax.random.normal(k, shape, f32)

    L = DEPTH
    return {
        'x': nrm(ks[0], (BATCH, SEQ, D_MODEL), 1.0),
        'attn_norm_g': gain(ks[1], (L, D_MODEL)),
        'w_in': nrm(ks[2], (L, D_MODEL, IN_COLS), D_MODEL ** -0.5),
        'qk_gain_a': gain(ks[3], (L, 2, HEAD_DIM)),
        'qk_gain_b': gain(ks[4], (L, 4, HEAD_DIM)),
        'rel_bias': nrm(ks[5], (N_HEADS, N_BUCKETS), 0.2),
        'cmp_pos': nrm(ks[6], (L, 2, CMP_BLOCK, HEAD_DIM), 0.02),
        'cmp_w1': nrm(ks[7], (L, 2, CMP_BLOCK * HEAD_DIM, CMP_HIDDEN), (CMP_BLOCK * HEAD_DIM) ** -0.5),
        'cmp_w2': nrm(ks[8], (L, 2, CMP_HIDDEN, HEAD_DIM), CMP_HIDDEN ** -0.5),
        'out_norm_g': gain(ks[9], (L, N_HEADS, HEAD_DIM)),
        'w_out': nrm(ks[10], (L, MIX_WIDTH, D_MODEL), MIX_WIDTH ** -0.5),
        'ffn_norm_g': gain(ks[11], (L, D_MODEL)),
        'peer_w_query': nrm(ks[12], (L, D_MODEL, PEER_HEADS * PEER_DQ), D_MODEL ** -0.5),
        'peer_sub_keys': nrm(ks[13], (L, PEER_HEADS, 2, N_KEYS, PEER_DQ // 2), (PEER_DQ // 2) ** -0.5),
        'peer_down': nrm(ks[14], (L, N_EXPERTS, D_MODEL), D_MODEL ** -0.5),
        'peer_up': nrm(ks[15], (L, N_EXPERTS, D_MODEL), 0.5),
    }


def reference(x, attn_norm_g, w_in, qk_gain_a, qk_gain_b, rel_bias, cmp_pos, cmp_w1, cmp_w2,
              out_norm_g, w_out, ffn_norm_g, peer_w_query, peer_sub_keys, peer_down, peer_up):
    b, s, _ = x.shape
    hd = HEAD_DIM
    sizes = [N_HEADS_A * hd] * 3 + [N_HEADS_B * hd] + [N_KV_B * hd] * (2 * N_BRANCH_B) + [N_BRANCH_B * N_HEADS_B]
    offsets = []
    acc = 0
    for sz in sizes[:-1]:
        acc += sz
        offsets.append(acc)
    h = x
    for layer in range(DEPTH):
        xn = rms_norm(h, attn_norm_g[layer])
        proj = xn @ w_in[layer]
        qa, ka, va, qb, kcr, vcr, ksl, vsl, kwi, vwi, gl = jnp.split(proj, offsets, axis=-1)
        qa = rms_norm(qa.reshape(b, s, N_HEADS_A, hd), qk_gain_a[layer, 0])
        ka = rms_norm(ka.reshape(b, s, N_HEADS_A, hd), qk_gain_a[layer, 1])
        va = va.reshape(b, s, N_HEADS_A, hd)
        o_a = dilated_attention(qa, ka, va, rel_bias[:N_HEADS_A])
        kv_shape = (b, s, N_KV_B, hd)
        qb = rms_norm(qb.reshape(b, s, N_HEADS_B, hd), qk_gain_b[layer, 0])
        ksl = rms_norm(ksl.reshape(kv_shape), qk_gain_b[layer, 2])
        kwi = rms_norm(kwi.reshape(kv_shape), qk_gain_b[layer, 3])
        o_b = nsa_attention(qb, kcr.reshape(kv_shape), vcr.reshape(kv_shape), ksl, vsl.reshape(kv_shape),
                            kwi, vwi.reshape(kv_shape), gl.reshape(b, s, N_HEADS_B, N_BRANCH_B),
                            qk_gain_b[layer, 1], cmp_pos[layer], cmp_w1[layer], cmp_w2[layer],
                            rel_bias[N_HEADS_A:])
        o = rms_norm(jnp.concatenate([o_a, o_b], axis=2), out_norm_g[layer]).reshape(b, s, MIX_WIDTH)
        h = h + o @ w_out[layer]
        h = h + peer_ffn(rms_norm(h, ffn_norm_g[layer]), peer_w_query[layer], peer_sub_keys[layer],
                         peer_down[layer], peer_up[layer])
    return h
```

```python
import functools
import math

import jax
import jax.numpy as jnp
import numpy as np
from jax import lax
from jax.experimental import pallas as pl
from jax.experimental.pallas import tpu as pltpu

F32 = jnp.float32
BF16 = jnp.bfloat16

HEAD_DIM = 128
LANES = 128
N_HEADS_A = 16
N_HEADS_B = 16
N_KV_B = 4
Q_PER_KV_B = 4
DILATED_PATTERNS = ((128, 1), (512, 4), (2048, 16))
CMP_BLOCK = 32
CMP_STRIDE = 16
SLC_BLOCK = 64
N_SLC = 16
WIN_B = 512
N_BUCKETS = 32
MAX_DISTANCE = 2048
PEER_HEADS = 8
N_KEYS = 128
PEER_TOPK = 16
RMS_EPS = 1e-6
NEG = -1e30
BIG = 3e38

COL_QA, COL_KA, COL_VA, COL_QB = 0, 16, 32, 48
COL_KCMP, COL_VCMP, COL_KSLC, COL_VSLC, COL_KWIN, COL_VWIN = 64, 68, 72, 76, 80, 84
N_PROJ_HEADS = 88
PROJ_COLS = N_PROJ_HEADS * HEAD_DIM

BAND_TQ = 128
CMP_TQ = 256
CMP_PAD = 112
CMP_BAND = 128
SLC_NEAR = 7
VMEM_LIMIT = 56 * 1024 * 1024


def _cparams(sem):
    return pltpu.CompilerParams(dimension_semantics=sem, vmem_limit_bytes=VMEM_LIMIT)


def _rel_bucket(dist):
    n = jnp.maximum(dist, 0)
    max_exact = N_BUCKETS // 2
    nf = jnp.maximum(n, 1).astype(F32)
    log_part = jnp.log(nf / max_exact) / math.log(MAX_DISTANCE / max_exact) * (N_BUCKETS - max_exact)
    large = jnp.minimum(max_exact + log_part.astype(jnp.int32), N_BUCKETS - 1)
    return jnp.where(n < max_exact, n, large)


def _bias_of_rel(table, rel):
    b = jnp.take(table.astype(F32), _rel_bucket(rel), axis=1)
    return jnp.where(rel >= 0, b, NEG)


def _rmsnorm_kernel(x_ref, g_ref, o_ref):
    x = x_ref[...]
    ms = jnp.mean(x * x, axis=-1, keepdims=True)
    o_ref[...] = (x * lax.rsqrt(ms + RMS_EPS) * g_ref[...]).astype(o_ref.dtype)


def _rmsnorm(x, g, tm=256):
    s, d = x.shape
    return pl.pallas_call(
        _rmsnorm_kernel,
        grid=(s // tm,),
        in_specs=[pl.BlockSpec((tm, d), lambda i: (i, 0)), pl.BlockSpec((1, d), lambda i: (0, 0))],
        out_specs=pl.BlockSpec((tm, d), lambda i: (i, 0)),
        out_shape=jax.ShapeDtypeStruct((s, d), BF16),
        compiler_params=_cparams(("parallel",)),
        name="rmsnorm",
    )(x, g.reshape(1, d).astype(F32))


def _proj_kernel(x_ref, w_ref, gain_ref, mode_ref, o_ref, *, n_chunks):
    acc = jnp.dot(x_ref[...], w_ref[...], preferred_element_type=F32)
    for c in range(n_chunks):
        sl = slice(c * LANES, (c + 1) * LANES)
        t = acc[:, sl]
        ms = jnp.mean(t * t, axis=-1, keepdims=True)
        mult = jnp.where(mode_ref[:, sl] > 0, lax.rsqrt(ms + RMS_EPS) * gain_ref[:, sl], 1.0)
        o_ref[:, sl] = (t * mult).astype(o_ref.dtype)


def _proj_headnorm(xn, w, gain, mode, tm, tn):
    s, d = xn.shape
    n = w.shape[1]
    return pl.pallas_call(
        functools.partial(_proj_kernel, n_chunks=tn // LANES),
        grid=(s // tm, n // tn),
        in_specs=[pl.BlockSpec((tm, d), lambda i, j: (i, 0)),
                  pl.BlockSpec((d, tn), lambda i, j: (0, j)),
                  pl.BlockSpec((1, tn), lambda i, j: (0, j)),
                  pl.BlockSpec((1, tn), lambda i, j: (0, j))],
        out_specs=pl.BlockSpec((tm, tn), lambda i, j: (i, j)),
        out_shape=jax.ShapeDtypeStruct((s, n), BF16),
        compiler_params=_cparams(("parallel", "arbitrary")),
        name="proj_headnorm",
    )(xn, w, gain, mode)


def _matmul_kernel(x_ref, w_ref, o_ref):
    o_ref[...] = jnp.dot(x_ref[...], w_ref[...], preferred_element_type=F32).astype(o_ref.dtype)


def _matmul(x, w, out_dtype, tm, tn):
    s, d = x.shape
    n = w.shape[1]
    return pl.pallas_call(
        _matmul_kernel,
        grid=(s // tm, n // tn),
        in_specs=[pl.BlockSpec((tm, d), lambda i, j: (i, 0)), pl.BlockSpec((d, tn), lambda i, j: (0, j))],
        out_specs=pl.BlockSpec((tm, tn), lambda i, j: (i, j)),
        out_shape=jax.ShapeDtypeStruct((s, n), out_dtype),
        compiler_params=_cparams(("parallel", "arbitrary")),
        name="matmul",
    )(x, w)


def _matmul_res_kernel(x_ref, w_ref, r_ref, o_ref):
    o_ref[...] = r_ref[...] + jnp.dot(x_ref[...], w_ref[...], preferred_element_type=F32)


def _matmul_residual(x, w, res, tm, tn):
    s, d = x.shape
    n = w.shape[1]
    return pl.pallas_call(
        _matmul_res_kernel,
        grid=(s // tm, n // tn),
        in_specs=[pl.BlockSpec((tm, d), lambda i, j: (i, 0)),
                  pl.BlockSpec((d, tn), lambda i, j: (0, j)),
                  pl.BlockSpec((tm, tn), lambda i, j: (i, j))],
        out_specs=pl.BlockSpec((tm, tn), lambda i, j: (i, j)),
        out_shape=jax.ShapeDtypeStruct((s, n), F32),
        compiler_params=_cparams(("parallel", "arbitrary")),
        name="matmul_residual",
    )(x, w, res)


def _banded_kernel(*refs, hb, r, nprev, tq, with_lse):
    nk = nprev + 1
    q_ref = refs[0]
    k_refs = refs[1:1 + nk]
    v_refs = refs[1 + nk:1 + 2 * nk]
    bias_ref = refs[1 + 2 * nk]
    o_ref = refs[2 + 2 * nk]
    i = pl.program_id(2)
    span = nk * tq
    if with_lse:
        lse_ref = refs[3 + 2 * nk]
        lane = lax.broadcasted_iota(jnp.int32, (tq, LANES), 1)
        lse_mat = jnp.zeros((tq, LANES), F32)
    for g in range(hb // r):
        gs = slice(g * LANES, (g + 1) * LANES)
        heads = [g * r + rr for rr in range(r)]
        qs = [q_ref[:, h * LANES:(h + 1) * LANES] for h in heads]
        q = qs[0] if r == 1 else jnp.concatenate(qs, axis=0)
        parts = []
        for j in range(nk):
            sj = lax.dot_general(q, k_refs[j][:, gs], (((1,), (1,)), ((), ())), preferred_element_type=F32)
            if j < nprev:
                sj = sj + jnp.where(i >= nprev - j, 0.0, NEG)
            parts.append(sj)
        s = jnp.concatenate(parts, axis=1)
        s = s + bias_ref[g * r:(g + 1) * r].reshape(r * tq, span)
        m = jnp.max(s, axis=-1, keepdims=True)
        p = jnp.exp(s - m)
        den = jnp.sum(p, axis=-1, keepdims=True)
        pb = p.astype(BF16)
        o = jnp.dot(pb[:, 0:tq], v_refs[0][:, gs], preferred_element_type=F32)
        for j in range(1, nk):
            o = o + jnp.dot(pb[:, j * tq:(j + 1) * tq], v_refs[j][:, gs], preferred_element_type=F32)
        o = o / den
        for rr, h in enumerate(heads):
            o_ref[:, h * LANES:(h + 1) * LANES] = o[rr * tq:(rr + 1) * tq].astype(o_ref.dtype)
        if with_lse:
            lse_mat = jnp.where(lane == g, m + jnp.log(den), lse_mat)
    if with_lse:
        lse_ref[...] = lse_mat


def _banded_attention(view, bias, *, n_r, n_hb, hb, r, nprev, qcol, kcol, vcol, with_lse):
    tq = BAND_TQ
    length = view.shape[0]
    nk = nprev + 1
    qw = hb * LANES
    kw = (hb // r) * LANES

    def kmap(col, back):
        return lambda rr, hh, i: (jnp.maximum(i - back, 0), col(rr, hh))

    in_specs = [pl.BlockSpec((tq, qw), lambda rr, hh, i: (i, qcol(rr, hh)))]
    in_specs += [pl.BlockSpec((tq, kw), kmap(kcol, nprev - j)) for j in range(nk)]
    in_specs += [pl.BlockSpec((tq, kw), kmap(vcol, nprev - j)) for j in range(nk)]
    in_specs += [pl.BlockSpec((None, hb, tq, nk * tq), lambda rr, hh, i: (hh, 0, 0, 0))]
    out_specs = [pl.BlockSpec((tq, qw), lambda rr, hh, i: (i, rr * n_hb + hh))]
    out_shape = [jax.ShapeDtypeStruct((length, n_r * n_hb * qw), BF16)]
    if with_lse:
        out_specs.append(pl.BlockSpec((tq, LANES), lambda rr, hh, i: (i, rr * n_hb + hh)))
        out_shape.append(jax.ShapeDtypeStruct((length, n_r * n_hb * LANES), F32))
    return pl.pallas_call(
        functools.partial(_banded_kernel, hb=hb, r=r, nprev=nprev, tq=tq, with_lse=with_lse),
        grid=(n_r, n_hb, length // tq),
        in_specs=in_specs,
        out_specs=out_specs,
        out_shape=out_shape,
        compiler_params=_cparams(("parallel", "parallel", "arbitrary")),
        name="banded_attention",
    )(*([view] * (1 + 2 * nk)), bias)


def _band_bias(table, max_dist, dist_scale, nprev):
    tq = BAND_TQ
    q_loc = jnp.arange(tq)[:, None]
    k_loc = jnp.arange((nprev + 1) * tq)[None, :] - nprev * tq
    rel = q_loc - k_loc
    b = _bias_of_rel(table, rel * dist_scale)
    return jnp.where((rel >= 0) & (rel <= max_dist), b, NEG)


def _compress_kernel(x_ref, w1_ref, w2_ref, pos_ref, gain_ref, o_ref, *, n_chunk):
    half = CMP_STRIDE
    a = jnp.zeros((n_chunk, HEAD_DIM), F32)
    b = jnp.zeros((n_chunk, HEAD_DIM), F32)
    for c in range(half):
        xc = x_ref[c]
        a = a + jnp.dot(xc, w1_ref[c * HEAD_DIM:(c + 1) * HEAD_DIM, :], preferred_element_type=F32)
        b = b + jnp.dot(xc, w1_ref[(half + c) * HEAD_DIM:(half + c + 1) * HEAD_DIM, :],
                        preferred_element_type=F32)
    pos = jnp.broadcast_to(pos_ref[...], (8, CMP_BLOCK * HEAD_DIM))
    posterm = jnp.dot(pos, w1_ref[...], preferred_element_type=F32)[0:1, :]
    pre = a + pltpu.roll(b, n_chunk - 1, 0) + posterm
    hid = jax.nn.gelu(pre)
    out = jnp.dot(hid.astype(BF16), w2_ref[...], preferred_element_type=F32)
    ms = jnp.mean(out * out, axis=-1, keepdims=True)
    normed = out * lax.rsqrt(ms + RMS_EPS) * gain_ref[...]
    o_ref[...] = jnp.where(pl.program_id(0) == 0, normed, out).astype(o_ref.dtype)


def _compress(xt, w1, w2, pos, gain):
    n_chunk = xt.shape[3]
    return pl.pallas_call(
        functools.partial(_compress_kernel, n_chunk=n_chunk),
        grid=(2, N_KV_B),
        in_specs=[pl.BlockSpec((None, None, CMP_STRIDE, n_chunk, HEAD_DIM), lambda w, g: (w, g, 0, 0, 0)),
                  pl.BlockSpec((None, CMP_BLOCK * HEAD_DIM, HEAD_DIM), lambda w, g: (w, 0, 0)),
                  pl.BlockSpec((None, HEAD_DIM, HEAD_DIM), lambda w, g: (w, 0, 0)),
                  pl.BlockSpec((None, 1, CMP_BLOCK * HEAD_DIM), lambda w, g: (w, 0, 0)),
                  pl.BlockSpec((1, HEAD_DIM), lambda w, g: (0, 0))],
        out_specs=pl.BlockSpec((None, None, n_chunk, HEAD_DIM), lambda w, g: (w, g, 0, 0)),
        out_shape=jax.ShapeDtypeStruct((2, N_KV_B, n_chunk, HEAD_DIM), BF16),
        compiler_params=_cparams(("parallel", "parallel")),
        name="nsa_compress",
    )(xt, w1, w2, pos, gain)


def _cmp_select_kernel(b31_ref, q_ref, kc_ref, vct_ref, bnear_ref, tt_ref, o_ref, mask_ref, p_scr, *, tq, n_rows):
    g = pl.program_id(0)
    n = pl.program_id(1)
    t0 = n * tq
    band_lo = pl.multiple_of(n * (tq // CMP_STRIDE), 16)
    rows = lax.broadcasted_iota(jnp.int32, (n_rows, tq), 0)
    far_ok = (rows >= CMP_PAD) & (rows < band_lo)
    brow = lax.broadcasted_iota(jnp.int32, (CMP_BAND, tq), 0) + band_lo
    band_pen = jnp.where(brow >= CMP_PAD, 0.0, NEG)
    t = t0 + lax.broadcasted_iota(jnp.int32, (1, tq), 1)
    valid = t >= CMP_BLOCK - 1
    kc_all = kc_ref[...]
    kc_band = kc_ref[pl.ds(band_lo, CMP_BAND), :]
    psum = jnp.zeros((n_rows, tq), F32)
    for r in range(Q_PER_KV_B):
        q = q_ref[:, r * LANES:(r + 1) * LANES]
        b31 = b31_ref[g * Q_PER_KV_B + r]
        s_far = lax.dot_general(kc_all, q, (((1,), (1,)), ((), ())), preferred_element_type=F32)
        s_far = jnp.where(far_ok, s_far + b31, NEG)
        s_band = lax.dot_general(kc_band, q, (((1,), (1,)), ((), ())), preferred_element_type=F32)
        s_band = s_band + bnear_ref[r] + band_pen
        m = jnp.maximum(jnp.max(s_far, axis=0, keepdims=True), jnp.max(s_band, axis=0, keepdims=True))
        p_far = jnp.exp(s_far - m)
        p_band = jnp.exp(s_band - m)
        den = jnp.sum(p_far, axis=0, keepdims=True) + jnp.sum(p_band, axis=0, keepdims=True)
        inv = jnp.where(valid, 1.0 / den, 0.0)
        p_scr[...] = p_far * inv
        p_scr[pl.ds(band_lo, CMP_BAND), :] = p_band * inv
        p = p_scr[...]
        o_t = jnp.dot(vct_ref[...], p.astype(BF16), preferred_element_type=F32)
        o_ref[:, r * LANES:(r + 1) * LANES] = o_t.T.astype(o_ref.dtype)
        psum = psum + p
    p_hi = psum.astype(BF16)
    p_lo = (psum - p_hi.astype(F32)).astype(BF16)
    imp = (jnp.dot(tt_ref[...], p_hi, preferred_element_type=F32)
           + jnp.dot(tt_ref[...], p_lo, preferred_element_type=F32))
    n_slc = imp.shape[0]
    jj = lax.broadcasted_iota(jnp.int32, (n_slc, tq), 0)
    cur = t // SLC_BLOCK
    allowed = jj <= cur
    score = jnp.where(jj == 0, 3e9,
                      jnp.where(jj == cur, 2e9,
                                jnp.where(jj == cur - 1, 1e9, jnp.where(allowed, imp, NEG))))
    work = score
    thr = jnp.zeros((1, tq), F32)
    for _ in range(N_SLC):
        thr = jnp.max(work, axis=0, keepdims=True)
        work = jnp.where(work >= thr, -BIG, work)
    mask_ref[...] = jnp.where((score >= thr) & allowed, 0.0, NEG)


def _cmp_select(proj, kcp, vcpt, bnear, b31, tt):
    s = proj.shape[0]
    tq = CMP_TQ
    n_rows = kcp.shape[1]
    n_slc = tt.shape[0]
    return pl.pallas_call(
        functools.partial(_cmp_select_kernel, tq=tq, n_rows=n_rows),
        grid=(N_KV_B, s // tq),
        in_specs=[pl.BlockSpec(memory_space=pltpu.SMEM),
                  pl.BlockSpec((tq, Q_PER_KV_B * LANES), lambda g, n: (n, COL_QB // Q_PER_KV_B + g)),
                  pl.BlockSpec((None, n_rows, HEAD_DIM), lambda g, n: (g, 0, 0)),
                  pl.BlockSpec((None, HEAD_DIM, n_rows), lambda g, n: (g, 0, 0)),
                  pl.BlockSpec((None, Q_PER_KV_B, CMP_BAND, tq), lambda g, n: (g, 0, 0, 0)),
                  pl.BlockSpec((n_slc, n_rows), lambda g, n: (0, 0))],
        out_specs=[pl.BlockSpec((tq, Q_PER_KV_B * LANES), lambda g, n: (n, g)),
                   pl.BlockSpec((None, n_slc, tq), lambda g, n: (g, 0, n))],
        out_shape=[jax.ShapeDtypeStruct((s, N_HEADS_B * HEAD_DIM), BF16),
                   jax.ShapeDtypeStruct((N_KV_B, n_slc, s), F32)],
        scratch_shapes=[pltpu.VMEM((n_rows, tq), F32)],
        compiler_params=_cparams(("parallel", "arbitrary")),
        name="nsa_cmp_select",
    )(b31, proj, kcp, vcpt, bnear, tt)


def _slc_kernel(b31_ref, q_ref, k_ref, vt_ref, mask_ref, btile_ref, o_ref, m_scr, l_scr, acc_scr, *, tq):
    g = pl.program_id(0)
    n = pl.program_id(1)
    blocks_per_chunk = tq // SLC_BLOCK
    m_scr[...] = jnp.full(m_scr.shape, -BIG, F32)
    l_scr[...] = jnp.zeros(l_scr.shape, F32)
    acc_scr[...] = jnp.zeros(acc_scr.shape, F32)
    qs = [q_ref[:, r * LANES:(r + 1) * LANES] for r in range(Q_PER_KV_B)]

    def chunk(c, near):
        k0 = pl.multiple_of(c * tq, tq)
        kch = k_ref[pl.ds(k0, tq), :]
        vch = vt_ref[:, pl.ds(k0, tq)]
        mrows = jnp.concatenate(
            [jnp.broadcast_to(mask_ref[pl.ds(c * blocks_per_chunk + b, 1), :], (SLC_BLOCK, tq))
             for b in range(blocks_per_chunk)], axis=0)
        for r in range(Q_PER_KV_B):
            s = lax.dot_general(kch, qs[r], (((1,), (1,)), ((), ())), preferred_element_type=F32)
            if near:
                s = s + mrows + btile_ref[r, n - c]
            else:
                s = s + (mrows + b31_ref[g * Q_PER_KV_B + r])
            m_old = m_scr[r]
            m_new = jnp.maximum(m_old, jnp.max(s, axis=0, keepdims=True))
            alpha = jnp.exp(m_old - m_new)
            p = jnp.exp(s - m_new)
            l_scr[r] = alpha * l_scr[r] + jnp.sum(p, axis=0, keepdims=True)
            acc_scr[r] = alpha * acc_scr[r] + jnp.dot(vch, p.astype(BF16), preferred_element_type=F32)
            m_scr[r] = m_new

    n_far = jnp.maximum(n - (SLC_NEAR - 1), 0)

    def far_body(c, carry):
        chunk(c, False)
        return carry

    def near_body(c, carry):
        chunk(c, True)
        return carry

    lax.fori_loop(0, n_far, far_body, 0)
    lax.fori_loop(n_far, n + 1, near_body, 0)
    for r in range(Q_PER_KV_B):
        o_t = acc_scr[r] / l_scr[r]
        o_ref[:, r * LANES:(r + 1) * LANES] = o_t.T.astype(o_ref.dtype)


def _slc_attention(proj, vt, mask, btile, b31):
    s = proj.shape[0]
    tq = CMP_TQ
    n_slc = mask.shape[1]
    return pl.pallas_call(
        functools.partial(_slc_kernel, tq=tq),
        grid=(N_KV_B, s // tq),
        in_specs=[pl.BlockSpec(memory_space=pltpu.SMEM),
                  pl.BlockSpec((tq, Q_PER_KV_B * LANES), lambda g, n: (n, COL_QB // Q_PER_KV_B + g)),
                  pl.BlockSpec((s, HEAD_DIM), lambda g, n: (0, COL_KSLC + g)),
                  pl.BlockSpec((None, HEAD_DIM, s), lambda g, n: (g, 0, 0)),
                  pl.BlockSpec((None, n_slc, tq), lambda g, n: (g, 0, n)),
                  pl.BlockSpec((None, Q_PER_KV_B, SLC_NEAR, tq, tq), lambda g, n: (g, 0, 0, 0, 0))],
        out_specs=pl.BlockSpec((tq, Q_PER_KV_B * LANES), lambda g, n: (n, g)),
        out_shape=jax.ShapeDtypeStruct((s, N_HEADS_B * HEAD_DIM), BF16),
        scratch_shapes=[pltpu.VMEM((Q_PER_KV_B, 1, tq), F32),
                        pltpu.VMEM((Q_PER_KV_B, 1, tq), F32),
                        pltpu.VMEM((Q_PER_KV_B, HEAD_DIM, tq), F32)],
        compiler_params=_cparams(("parallel", "arbitrary")),
        name="nsa_selected",
    )(b31, proj, proj, vt, mask, btile)


def _combine_kernel(o1_ref, o2_ref, o3_ref, l1_ref, l2_ref, l3_ref, oc_ref, os_ref, ow_ref, gl_ref, g_ref, o_ref):
    heads_per_lse_block = 8
    for h in range(N_HEADS_A):
        sl = slice(h * LANES, (h + 1) * LANES)
        col = (h // heads_per_lse_block) * LANES + h % heads_per_lse_block
        l1 = l1_ref[:, col:col + 1]
        l2 = l2_ref[:, col:col + 1]
        l3 = l3_ref[:, col:col + 1]
        m = jnp.maximum(jnp.maximum(l1, l2), l3)
        e1, e2, e3 = jnp.exp(l1 - m), jnp.exp(l2 - m), jnp.exp(l3 - m)
        den = e1 + e2 + e3
        o = (o1_ref[:, sl].astype(F32) * (e1 / den) + o2_ref[:, sl].astype(F32) * (e2 / den)
             + o3_ref[:, sl].astype(F32) * (e3 / den))
        ms = jnp.mean(o * o, axis=-1, keepdims=True)
        o_ref[:, sl] = (o * lax.rsqrt(ms + RMS_EPS) * g_ref[h:h + 1, :]).astype(o_ref.dtype)
    gates = jax.nn.sigmoid(gl_ref[...])
    for h in range(N_HEADS_B):
        sl = slice(h * LANES, (h + 1) * LANES)
        o = (gates[:, 3 * h:3 * h + 1] * oc_ref[:, sl].astype(F32)
             + gates[:, 3 * h + 1:3 * h + 2] * os_ref[:, sl].astype(F32)
             + gates[:, 3 * h + 2:3 * h + 3] * ow_ref[:, sl].astype(F32))
        ms = jnp.mean(o * o, axis=-1, keepdims=True)
        hh = N_HEADS_A + h
        o_ref[:, hh * LANES:(hh + 1) * LANES] = (
            o * lax.rsqrt(ms + RMS_EPS) * g_ref[hh:hh + 1, :]).astype(o_ref.dtype)


def _combine(o_dil, lse_dil, o_cmp, o_slc, o_win, gl, out_gain, tm=256):
    s = o_cmp.shape[0]
    wa = N_HEADS_A * HEAD_DIM
    wide = pl.BlockSpec((tm, wa), lambda i: (i, 0))
    lse_spec = pl.BlockSpec((tm, 2 * LANES), lambda i: (i, 0))
    return pl.pallas_call(
        _combine_kernel,
        grid=(s // tm,),
        in_specs=[wide, wide, wide, lse_spec, lse_spec, lse_spec, wide, wide, wide,
                  pl.BlockSpec((tm, LANES), lambda i: (i, 0)),
                  pl.BlockSpec((N_HEADS_A + N_HEADS_B, HEAD_DIM), lambda i: (0, 0))],
        out_specs=pl.BlockSpec((tm, 2 * wa), lambda i: (i, 0)),
        out_shape=jax.ShapeDtypeStruct((s, 2 * wa), BF16),
        compiler_params=_cparams(("parallel",)),
        name="combine_headnorm",
    )(*o_dil, *lse_dil, o_cmp, o_slc, o_win, gl, out_gain)


def _topk_rounds(work, k):
    vals = []
    for _ in range(k):
        mx = jnp.max(work, axis=0, keepdims=True)
        vals.append(mx)
        work = jnp.where(work >= mx, -BIG, work)
    return vals


def _peer_topk_kernel(q_ref, keys_ref, s1_ref, s2_ref, e1_ref, e2_ref, thr_ref, *, tm):
    half = N_KEYS
    dn = (((1,), (1,)), ((), ()))
    s1 = lax.dot_general(keys_ref[0], q_ref[:, 0:half], dn, preferred_element_type=F32)
    s2 = lax.dot_general(keys_ref[1], q_ref[:, half:2 * half], dn, preferred_element_type=F32)
    v1 = _topk_rounds(s1, PEER_TOPK)
    v2 = _topk_rounds(s2, PEER_TOPK)
    v2m = jnp.concatenate(v2, axis=0)
    cand = jnp.concatenate([v1[a] + v2m for a in range(PEER_TOPK)], axis=0)
    top = _topk_rounds(cand, PEER_TOPK)
    z = jnp.zeros((1, tm), F32)
    for tv in top:
        z = z + jnp.exp(tv - top[0])
    s1_ref[...] = s1
    s2_ref[...] = s2
    e1_ref[...] = jnp.exp(s1 - v1[0]) / z
    e2_ref[...] = jnp.exp(s2 - v2[0])
    thr_ref[...] = jnp.broadcast_to(top[PEER_TOPK - 1], (8, tm))


def _peer_topk(qp, keys, tm=256):
    s = qp.shape[0]
    stat = jax.ShapeDtypeStruct((PEER_HEADS, N_KEYS, s), F32)
    stat_spec = pl.BlockSpec((None, N_KEYS, tm), lambda n, h: (h, 0, n))
    return pl.pallas_call(
        functools.partial(_peer_topk_kernel, tm=tm),
        grid=(s // tm, PEER_HEADS),
        in_specs=[pl.BlockSpec((tm, 2 * N_KEYS), lambda n, h: (n, h)),
                  pl.BlockSpec((None, 2, N_KEYS, N_KEYS), lambda n, h: (h, 0, 0, 0))],
        out_specs=[stat_spec, stat_spec, stat_spec, stat_spec,
                   pl.BlockSpec((None, 8, tm), lambda n, h: (h, 0, n))],
        out_shape=[stat, stat, stat, stat, jax.ShapeDtypeStruct((PEER_HEADS, 8, s), F32)],
        compiler_params=_cparams(("parallel", "arbitrary")),
        name="peer_topk",
    )(qp, keys)


def _peer_dense_kernel(xn_ref, down_ref, up_ref, s1_ref, s2_ref, e1_ref, e2_ref, thr_ref, o_ref, *, tm, te):
    j = pl.program_id(1)

    @pl.when(j == 0)
    def _():
        o_ref[...] = jnp.zeros(o_ref.shape, F32)

    hid = lax.dot_general(down_ref[...], xn_ref[...], (((1,), (1,)), ((), ())), preferred_element_type=F32)
    act = jax.nn.gelu(hid)
    pieces = []
    for ii in range(te // N_KEYS):
        i1 = j * (te // N_KEYS) + ii
        w = jnp.zeros((N_KEYS, tm), F32)
        for h in range(PEER_HEADS):
            s1row = s1_ref[h, pl.ds(i1, 1), :]
            e1row = e1_ref[h, pl.ds(i1, 1), :]
            hit = (s1row + s2_ref[h]) >= thr_ref[h, 0:1, :]
            w = w + jnp.where(hit, e2_ref[h], 0.0) * e1row
        pieces.append(w * act[ii * N_KEYS:(ii + 1) * N_KEYS])
    a_t = pieces[0] if len(pieces) == 1 else jnp.concatenate(pieces, axis=0)
    a = a_t.T.astype(BF16)
    o_ref[...] += jnp.dot(a, up_ref[...], preferred_element_type=F32)


def _peer_dense(xn, down, up, s1, s2, e1, e2, thr, tm, te):
    s, d = xn.shape
    n_exp = down.shape[0]
    stat_spec = pl.BlockSpec((PEER_HEADS, N_KEYS, tm), lambda n, j: (0, 0, n))
    return pl.pallas_call(
        functools.partial(_peer_dense_kernel, tm=tm, te=te),
        grid=(s // tm, n_exp // te),
        in_specs=[pl.BlockSpec((tm, d), lambda n, j: (n, 0)),
                  pl.BlockSpec((te, d), lambda n, j: (j, 0)),
                  pl.BlockSpec((te, d), lambda n, j: (j, 0)),
                  stat_spec, stat_spec, stat_spec, stat_spec,
                  pl.BlockSpec((PEER_HEADS, 8, tm), lambda n, j: (0, 0, n))],
        out_specs=pl.BlockSpec((tm, d), lambda n, j: (n, 0)),
        out_shape=jax.ShapeDtypeStruct((s, d), F32),
        compiler_params=_cparams(("parallel", "arbitrary")),
        name="peer_dense",
    )(xn, down, up, s1, s2, e1, e2, thr)


def _add_kernel(a_ref, b_ref, o_ref):
    o_ref[...] = a_ref[...] + b_ref[...]


def _add(a, b, tm=256):
    s, d = a.shape
    spec = pl.BlockSpec((tm, d), lambda i: (i, 0))
    return pl.pallas_call(
        _add_kernel, grid=(s // tm,), in_specs=[spec, spec], out_specs=spec,
        out_shape=jax.ShapeDtypeStruct((s, d), F32),
        compiler_params=_cparams(("parallel",)), name="residual_add",
    )(a, b)


def _tile(n, pref):
    t = pref
    while n % t:
        t //= 2
    return t


def _layer(x, attn_norm_g, w_in, qk_gain_a, qk_gain_b, rel_bias, cmp_pos, cmp_w1, cmp_w2,
           out_norm_g, w_out, ffn_norm_g, peer_w_query, peer_sub_keys, peer_down, peer_up):
    s, d = x.shape
    scale = HEAD_DIM ** -0.5
    ones = jnp.ones((HEAD_DIM,), F32)
    zeros = jnp.zeros((HEAD_DIM,), F32)

    xn = _rmsnorm(x, attn_norm_g)
    head_gain = ([qk_gain_a[0] * scale] * 16 + [qk_gain_a[1]] * 16 + [ones] * 16 + [qk_gain_b[0] * scale] * 16
                 + [ones] * 8 + [qk_gain_b[2]] * 4 + [ones] * 4 + [qk_gain_b[3]] * 4 + [ones] * 4)
    head_mode = ([ones] * 32 + [zeros] * 16 + [ones] * 16 + [zeros] * 8 + [ones] * 4 + [zeros] * 4
                 + [ones] * 4 + [zeros] * 4)
    gain = jnp.concatenate(head_gain).astype(F32).reshape(1, PROJ_COLS)
    mode = jnp.concatenate(head_mode).reshape(1, PROJ_COLS)
    w_main = w_in[:, :PROJ_COLS].astype(BF16)
    n_gate = w_in.shape[1] - PROJ_COLS
    w_gate = jnp.pad(w_in[:, PROJ_COLS:], ((0, 0), (0, LANES - n_gate))).astype(BF16)
    tm = _tile(s, 1024)
    proj = _proj_headnorm(xn, w_main, gain, mode, tm, 1024)
    gl = _matmul(xn, w_gate, F32, tm, LANES)

    table_a = rel_bias[:N_HEADS_A]
    table_b = rel_bias[N_HEADS_A:]

    o_dil, lse_dil = [], []
    hb = 8
    for window, dil in DILATED_PATTERNS:
        length = s // dil
        view = proj.reshape(length, dil * PROJ_COLS)
        bias = _band_bias(table_a, window // dil, dil, 1).reshape(N_HEADS_A // hb, hb, BAND_TQ, 2 * BAND_TQ)
        blocks = PROJ_COLS // (hb * LANES)
        o, lse = _banded_attention(
            view, bias, n_r=dil, n_hb=N_HEADS_A // hb, hb=hb, r=1, nprev=1,
            qcol=lambda rr, hh, _b=blocks: rr * _b + COL_QA // hb + hh,
            kcol=lambda rr, hh, _b=blocks: rr * _b + COL_KA // hb + hh,
            vcol=lambda rr, hh, _b=blocks: rr * _b + COL_VA // hb + hh,
            with_lse=True)
        o_dil.append(o.reshape(s, N_HEADS_A * HEAD_DIM))
        lse_dil.append(lse.reshape(s, (N_HEADS_A // hb) * LANES))

    nprev_w = -(-(WIN_B - 1) // BAND_TQ)
    bias_w = _band_bias(table_b, WIN_B - 1, 1, nprev_w).reshape(N_KV_B, Q_PER_KV_B, BAND_TQ, (nprev_w + 1) * BAND_TQ)
    o_win = _banded_attention(
        proj, bias_w, n_r=1, n_hb=N_KV_B, hb=Q_PER_KV_B, r=Q_PER_KV_B, nprev=nprev_w,
        qcol=lambda rr, hh: COL_QB // Q_PER_KV_B + hh,
        kcol=lambda rr, hh: COL_KWIN + hh,
        vcol=lambda rr, hh: COL_VWIN + hh,
        with_lse=False)[0]

    n_chunk = s // CMP_STRIDE
    n_cmp = (s - CMP_BLOCK) // CMP_STRIDE + 1
    raw = proj[:, COL_KCMP * HEAD_DIM:(COL_VCMP + N_KV_B) * HEAD_DIM]
    xt = raw.reshape(n_chunk, CMP_STRIDE, 2, N_KV_B, HEAD_DIM).transpose(2, 3, 1, 0, 4)
    kv_c = _compress(xt, cmp_w1.astype(BF16), cmp_w2.astype(BF16),
                     cmp_pos.reshape(2, 1, CMP_BLOCK * HEAD_DIM).astype(BF16),
                     qk_gain_b[1].reshape(1, HEAD_DIM).astype(F32))
    n_rows = CMP_PAD + n_chunk + 16
    kv_c = jnp.pad(kv_c[:, :, :n_cmp], ((0, 0), (0, 0), (CMP_PAD, n_rows - CMP_PAD - n_cmp), (0, 0)))
    kcp = kv_c[0]
    vcpt = kv_c[1].transpose(0, 2, 1)

    tl = jnp.arange(CMP_TQ)[None, :]
    il = jnp.arange(CMP_BAND)[:, None]
    rel_near = CMP_STRIDE * CMP_PAD - (CMP_BLOCK - 1) + tl - CMP_STRIDE * il
    bnear = _bias_of_rel(table_b, rel_near).reshape(N_KV_B, Q_PER_KV_B, CMP_BAND, CMP_TQ)
    b31 = table_b[:, N_BUCKETS - 1].astype(F32)
    n_slc = s // SLC_BLOCK
    ratio = SLC_BLOCK // CMP_STRIDE
    tt_np = np.zeros((n_slc, n_rows), np.float32)
    for jblk in range(n_slc):
        for off, wgt in ((-1, 1.0), (0, 2.0), (1, 2.0), (2, 2.0), (3, 1.0)):
            i_c = ratio * jblk + off
            if 0 <= i_c < n_cmp:
                tt_np[jblk, CMP_PAD + i_c] = wgt
    tt = jnp.asarray(tt_np, BF16)
    o_cmp, mask = _cmp_select(proj, kcp, vcpt, bnear, b31, tt)

    vt = proj[:, COL_VSLC * HEAD_DIM:(COL_VSLC + N_KV_B) * HEAD_DIM].reshape(s, N_KV_B, HEAD_DIM).transpose(1, 2, 0)
    kl = jnp.arange(CMP_TQ)[:, None]
    od = jnp.arange(SLC_NEAR)[:, None, None]
    rel_s = CMP_TQ * od + tl[None] - kl[None]
    btile = _bias_of_rel(table_b, rel_s).reshape(N_KV_B, Q_PER_KV_B, SLC_NEAR, CMP_TQ, CMP_TQ)
    o_slc = _slc_attention(proj, vt, mask, btile, b31)

    o_n = _combine(o_dil, lse_dil, o_cmp, o_slc, o_win, gl, out_norm_g.astype(F32))
    h = _matmul_residual(o_n, w_out.astype(BF16), x, tm, _tile(d, 1024))

    xn2 = _rmsnorm(h, ffn_norm_g)
    qp = _matmul(xn2, peer_w_query.astype(BF16), BF16, tm, 1024)
    s1, s2, e1, e2, thr = _peer_topk(qp, peer_sub_keys.astype(BF16))
    peer = _peer_dense(xn2, peer_down.astype(BF16), peer_up.astype(BF16), s1, s2, e1, e2, thr,
                       _tile(s, 512), 256)
    return _add(h, peer)


def kernel(x, attn_norm_g, w_in, qk_gain_a, qk_gain_b, rel_bias, cmp_pos, cmp_w1, cmp_w2, out_norm_g, w_out,
           ffn_norm_g, peer_w_query, peer_sub_keys, peer_down, peer_up):
    b = x.shape[0]
    outs = [_layer(x[bi], attn_norm_g[0], w_in[0], qk_gain_a[0], qk_gain_b[0], rel_bias, cmp_pos[0], cmp_w1[0],
                   cmp_w2[0], out_norm_g[0], w_out[0], ffn_norm_g[0], peer_w_query[0], peer_sub_keys[0],
                   peer_down[0], peer_up[0]) for bi in range(b)]
    return jnp.stack(outs, axis=0)
```

```python
import functools
import math

import jax
import jax.numpy as jnp
import numpy as np
from jax import lax
from jax.experimental import pallas as pl
from jax.experimental.pallas import tpu as pltpu

F32 = jnp.float32
BF16 = jnp.bfloat16

HEAD_DIM = 128
LANES = 128
N_HEADS_A = 16
N_HEADS_B = 16
N_KV_B = 4
Q_PER_KV_B = 4
DILATED_PATTERNS = ((128, 1), (512, 4), (2048, 16))
CMP_BLOCK = 32
CMP_STRIDE = 16
SLC_BLOCK = 64
N_SLC = 16
WIN_B = 512
N_BUCKETS = 32
MAX_DISTANCE = 2048
PEER_HEADS = 8
N_KEYS = 128
PEER_TOPK = 16
RMS_EPS = 1e-6
NEG = -1e30
BIG = 3e38

COL_QA, COL_KA, COL_VA, COL_QB = 0, 16, 32, 48
COL_KCMP, COL_VCMP, COL_KSLC, COL_VSLC, COL_KWIN, COL_VWIN = 64, 68, 72, 76, 80, 84
N_PROJ_HEADS = 88
PROJ_COLS = N_PROJ_HEADS * HEAD_DIM

BAND_TQ = 128
CMP_TQ = 256
CMP_PAD = 112
CMP_BAND = 128
SLC_NEAR = 8
VMEM_LIMIT = 56 * 1024 * 1024


def _cparams(sem):
    return pltpu.CompilerParams(dimension_semantics=sem, vmem_limit_bytes=VMEM_LIMIT)


def _rel_bucket(dist):
    n = np.maximum(dist, 0)
    max_exact = N_BUCKETS // 2
    nf = np.maximum(n, 1).astype(np.float32)
    log_part = (np.log(nf / np.float32(max_exact)) / np.float32(math.log(MAX_DISTANCE / max_exact))
                * np.float32(N_BUCKETS - max_exact))
    large = np.minimum(max_exact + log_part.astype(np.int32), N_BUCKETS - 1)
    return np.where(n < max_exact, n, large).astype(np.int32)


def _bias_of_rel(table, rel):
    rel = np.asarray(rel)
    bucket = jnp.asarray(np.where(rel >= 0, _rel_bucket(rel), -1))[None]
    table = table.astype(F32).reshape((table.shape[0], N_BUCKETS) + (1,) * rel.ndim)
    out = jnp.full((table.shape[0],) + rel.shape, NEG, F32)
    for k in range(N_BUCKETS):
        out = jnp.where(bucket == k, table[:, k], out)
    return out


def _rmsnorm_kernel(x_ref, g_ref, o_ref):
    x = x_ref[...]
    ms = jnp.mean(x * x, axis=-1, keepdims=True)
    o_ref[...] = (x * lax.rsqrt(ms + RMS_EPS) * g_ref[...]).astype(o_ref.dtype)


def _rmsnorm(x, g, tm=256):
    s, d = x.shape
    return pl.pallas_call(
        _rmsnorm_kernel,
        grid=(s // tm,),
        in_specs=[pl.BlockSpec((tm, d), lambda i: (i, 0)), pl.BlockSpec((1, d), lambda i: (0, 0))],
        out_specs=pl.BlockSpec((tm, d), lambda i: (i, 0)),
        out_shape=jax.ShapeDtypeStruct((s, d), BF16),
        compiler_params=_cparams(("parallel",)),
        name="rmsnorm",
    )(x, g.reshape(1, d).astype(F32))


def _proj_kernel(x_ref, w_ref, gain_ref, mode_ref, o_ref, *, n_chunks):
    acc = jnp.dot(x_ref[...], w_ref[...], preferred_element_type=F32)
    for c in range(n_chunks):
        sl = slice(c * LANES, (c + 1) * LANES)
        t = acc[:, sl]
        ms = jnp.mean(t * t, axis=-1, keepdims=True)
        mult = jnp.where(mode_ref[:, sl] > 0, lax.rsqrt(ms + RMS_EPS) * gain_ref[:, sl], 1.0)
        o_ref[:, sl] = (t * mult).astype(o_ref.dtype)


def _proj_headnorm(xn, w, gain, mode, tm, tn):
    s, d = xn.shape
    n = w.shape[1]
    return pl.pallas_call(
        functools.partial(_proj_kernel, n_chunks=tn // LANES),
        grid=(s // tm, n // tn),
        in_specs=[pl.BlockSpec((tm, d), lambda i, j: (i, 0)),
                  pl.BlockSpec((d, tn), lambda i, j: (0, j)),
                  pl.BlockSpec((1, tn), lambda i, j: (0, j)),
                  pl.BlockSpec((1, tn), lambda i, j: (0, j))],
        out_specs=pl.BlockSpec((tm, tn), lambda i, j: (i, j)),
        out_shape=jax.ShapeDtypeStruct((s, n), BF16),
        compiler_params=_cparams(("parallel", "arbitrary")),
        name="proj_headnorm",
    )(xn, w, gain, mode)


def _matmul_kernel(x_ref, w_ref, o_ref):
    o_ref[...] = jnp.dot(x_ref[...], w_ref[...], preferred_element_type=F32).astype(o_ref.dtype)


def _matmul(x, w, out_dtype, tm, tn):
    s, d = x.shape
    n = w.shape[1]
    return pl.pallas_call(
        _matmul_kernel,
        grid=(s // tm, n // tn),
        in_specs=[pl.BlockSpec((tm, d), lambda i, j: (i, 0)), pl.BlockSpec((d, tn), lambda i, j: (0, j))],
        out_specs=pl.BlockSpec((tm, tn), lambda i, j: (i, j)),
        out_shape=jax.ShapeDtypeStruct((s, n), out_dtype),
        compiler_params=_cparams(("parallel", "arbitrary")),
        name="matmul",
    )(x, w)


def _matmul_res_kernel(x_ref, w_ref, r_ref, o_ref):
    o_ref[...] = r_ref[...] + jnp.dot(x_ref[...], w_ref[...], preferred_element_type=F32)


def _matmul_residual(x, w, res, tm, tn):
    s, d = x.shape
    n = w.shape[1]
    return pl.pallas_call(
        _matmul_res_kernel,
        grid=(s // tm, n // tn),
        in_specs=[pl.BlockSpec((tm, d), lambda i, j: (i, 0)),
                  pl.BlockSpec((d, tn), lambda i, j: (0, j)),
                  pl.BlockSpec((tm, tn), lambda i, j: (i, j))],
        out_specs=pl.BlockSpec((tm, tn), lambda i, j: (i, j)),
        out_shape=jax.ShapeDtypeStruct((s, n), F32),
        compiler_params=_cparams(("parallel", "arbitrary")),
        name="matmul_residual",
    )(x, w, res)


def _banded_kernel(*refs, hb, r, nprev, tq, with_lse):
    nk = nprev + 1
    q_ref = refs[0]
    k_refs = refs[1:1 + nk]
    v_refs = refs[1 + nk:1 + 2 * nk]
    bias_ref = refs[1 + 2 * nk]
    o_ref = refs[2 + 2 * nk]
    i = pl.program_id(2)
    span = nk * tq
    if with_lse:
        lse_ref = refs[3 + 2 * nk]
        lane = lax.broadcasted_iota(jnp.int32, (tq, LANES), 1)
        lse_mat = jnp.zeros((tq, LANES), F32)
    for g in range(hb // r):
        gs = slice(g * LANES, (g + 1) * LANES)
        heads = [g * r + rr for rr in range(r)]
        qs = [q_ref[:, h * LANES:(h + 1) * LANES] for h in heads]
        q = qs[0] if r == 1 else jnp.concatenate(qs, axis=0)
        parts = []
        for j in range(nk):
            sj = lax.dot_general(q, k_refs[j][:, gs], (((1,), (1,)), ((), ())), preferred_element_type=F32)
            if j < nprev:
                sj = sj + jnp.where(i >= nprev - j, 0.0, NEG)
            parts.append(sj)
        s = jnp.concatenate(parts, axis=1)
        s = s + bias_ref[g * r:(g + 1) * r].reshape(r * tq, span)
        m = jnp.max(s, axis=-1, keepdims=True)
        p = jnp.exp(s - m)
        den = jnp.sum(p, axis=-1, keepdims=True)
        pb = p.astype(BF16)
        o = jnp.dot(pb[:, 0:tq], v_refs[0][:, gs], preferred_element_type=F32)
        for j in range(1, nk):
            o = o + jnp.dot(pb[:, j * tq:(j + 1) * tq], v_refs[j][:, gs], preferred_element_type=F32)
        o = o / den
        for rr, h in enumerate(heads):
            o_ref[:, h * LANES:(h + 1) * LANES] = o[rr * tq:(rr + 1) * tq].astype(o_ref.dtype)
        if with_lse:
            lse_mat = jnp.where(lane == g, m + jnp.log(den), lse_mat)
    if with_lse:
        lse_ref[...] = lse_mat


def _banded_attention(view, bias, *, n_r, n_hb, hb, r, nprev, qcol, kcol, vcol, with_lse):
    tq = BAND_TQ
    length = view.shape[0]
    nk = nprev + 1
    qw = hb * LANES
    kw = (hb // r) * LANES

    def kmap(col, back):
        return lambda rr, hh, i: (jnp.maximum(i - back, 0), col(rr, hh))

    in_specs = [pl.BlockSpec((tq, qw), lambda rr, hh, i: (i, qcol(rr, hh)))]
    in_specs += [pl.BlockSpec((tq, kw), kmap(kcol, nprev - j)) for j in range(nk)]
    in_specs += [pl.BlockSpec((tq, kw), kmap(vcol, nprev - j)) for j in range(nk)]
    in_specs += [pl.BlockSpec((None, hb, tq, nk * tq), lambda rr, hh, i: (hh, 0, 0, 0))]
    out_specs = [pl.BlockSpec((tq, qw), lambda rr, hh, i: (i, rr * n_hb + hh))]
    out_shape = [jax.ShapeDtypeStruct((length, n_r * n_hb * qw), BF16)]
    if with_lse:
        out_specs.append(pl.BlockSpec((tq, LANES), lambda rr, hh, i: (i, rr * n_hb + hh)))
        out_shape.append(jax.ShapeDtypeStruct((length, n_r * n_hb * LANES), F32))
    return pl.pallas_call(
        functools.partial(_banded_kernel, hb=hb, r=r, nprev=nprev, tq=tq, with_lse=with_lse),
        grid=(n_r, n_hb, length // tq),
        in_specs=in_specs,
        out_specs=out_specs,
        out_shape=out_shape,
        compiler_params=_cparams(("parallel", "parallel", "arbitrary")),
        name="banded_attention",
    )(*([view] * (1 + 2 * nk)), bias)


def _band_bias(table, max_dist, dist_scale, nprev):
    tq = BAND_TQ
    q_loc = np.arange(tq)[:, None]
    k_loc = np.arange((nprev + 1) * tq)[None, :] - nprev * tq
    rel = q_loc - k_loc
    return _bias_of_rel(table, np.where((rel >= 0) & (rel <= max_dist), rel * dist_scale, -1))


def _compress_kernel(x_ref, w1_ref, w2_ref, pos_ref, gain_ref, o_ref, *, n_chunk):
    half = CMP_STRIDE
    a = jnp.zeros((n_chunk, HEAD_DIM), F32)
    b = jnp.zeros((n_chunk, HEAD_DIM), F32)
    for c in range(half):
        xc = x_ref[c]
        a = a + jnp.dot(xc, w1_ref[c * HEAD_DIM:(c + 1) * HEAD_DIM, :], preferred_element_type=F32)
        b = b + jnp.dot(xc, w1_ref[(half + c) * HEAD_DIM:(half + c + 1) * HEAD_DIM, :],
                        preferred_element_type=F32)
    pos = jnp.broadcast_to(pos_ref[...], (8, CMP_BLOCK * HEAD_DIM))
    posterm = jnp.dot(pos, w1_ref[...], preferred_element_type=F32)[0:1, :]
    pre = a + pltpu.roll(b, n_chunk - 1, 0) + posterm
    hid = jax.nn.gelu(pre)
    out = jnp.dot(hid.astype(BF16), w2_ref[...], preferred_element_type=F32)
    ms = jnp.mean(out * out, axis=-1, keepdims=True)
    normed = out * lax.rsqrt(ms + RMS_EPS) * gain_ref[...]
    o_ref[...] = jnp.where(pl.program_id(0) == 0, normed, out).astype(o_ref.dtype)


def _compress(xt, w1, w2, pos, gain):
    n_chunk = xt.shape[3]
    return pl.pallas_call(
        functools.partial(_compress_kernel, n_chunk=n_chunk),
        grid=(2, N_KV_B),
        in_specs=[pl.BlockSpec((None, None, CMP_STRIDE, n_chunk, HEAD_DIM), lambda w, g: (w, g, 0, 0, 0)),
                  pl.BlockSpec((None, CMP_BLOCK * HEAD_DIM, HEAD_DIM), lambda w, g: (w, 0, 0)),
                  pl.BlockSpec((None, HEAD_DIM, HEAD_DIM), lambda w, g: (w, 0, 0)),
                  pl.BlockSpec((None, 1, CMP_BLOCK * HEAD_DIM), lambda w, g: (w, 0, 0)),
                  pl.BlockSpec((1, HEAD_DIM), lambda w, g: (0, 0))],
        out_specs=pl.BlockSpec((None, None, n_chunk, HEAD_DIM), lambda w, g: (w, g, 0, 0)),
        out_shape=jax.ShapeDtypeStruct((2, N_KV_B, n_chunk, HEAD_DIM), BF16),
        compiler_params=_cparams(("parallel", "parallel")),
        name="nsa_compress",
    )(xt, w1, w2, pos, gain)


def _cmp_select_kernel(b31_ref, q_ref, kc_ref, vct_ref, bnear_ref, tt_ref, o_ref, mask_ref, p_scr, *, tq, n_rows):
    g = pl.program_id(0)
    n = pl.program_id(1)
    t0 = n * tq
    band_lo = pl.multiple_of(n * (tq // CMP_STRIDE), 16)
    rows = lax.broadcasted_iota(jnp.int32, (n_rows, tq), 0)
    far_ok = (rows >= CMP_PAD) & (rows < band_lo)
    brow = lax.broadcasted_iota(jnp.int32, (CMP_BAND, tq), 0) + band_lo
    band_pen = jnp.where(brow >= CMP_PAD, 0.0, NEG)
    t = t0 + lax.broadcasted_iota(jnp.int32, (1, tq), 1)
    valid = t >= CMP_BLOCK - 1
    kc_all = kc_ref[...]
    kc_band = kc_ref[pl.ds(band_lo, CMP_BAND), :]
    psum = jnp.zeros((n_rows, tq), F32)
    for r in range(Q_PER_KV_B):
        q = q_ref[:, r * LANES:(r + 1) * LANES]
        b31 = b31_ref[g * Q_PER_KV_B + r]
        s_far = lax.dot_general(kc_all, q, (((1,), (1,)), ((), ())), preferred_element_type=F32)
        s_far = jnp.where(far_ok, s_far + b31, NEG)
        s_band = lax.dot_general(kc_band, q, (((1,), (1,)), ((), ())), preferred_element_type=F32)
        s_band = s_band + bnear_ref[r] + band_pen
        m = jnp.maximum(jnp.max(s_far, axis=0, keepdims=True), jnp.max(s_band, axis=0, keepdims=True))
        p_far = jnp.exp(s_far - m)
        p_band = jnp.exp(s_band - m)
        den = jnp.sum(p_far, axis=0, keepdims=True) + jnp.sum(p_band, axis=0, keepdims=True)
        inv = jnp.where(valid, 1.0 / den, 0.0)
        p_scr[...] = p_far * inv
        p_scr[pl.ds(band_lo, CMP_BAND), :] = p_band * inv
        p = p_scr[...]
        o_t = jnp.dot(vct_ref[...], p.astype(BF16), preferred_element_type=F32)
        o_ref[:, r * LANES:(r + 1) * LANES] = o_t.T.astype(o_ref.dtype)
        psum = psum + p
    p_hi = psum.astype(BF16)
    p_lo = (psum - p_hi.astype(F32)).astype(BF16)
    imp = (jnp.dot(tt_ref[...], p_hi, preferred_element_type=F32)
           + jnp.dot(tt_ref[...], p_lo, preferred_element_type=F32))
    n_slc = imp.shape[0]
    jj = lax.broadcasted_iota(jnp.int32, (n_slc, tq), 0)
    cur = t // SLC_BLOCK
    allowed = jj <= cur
    score = jnp.where(jj == 0, 3e9,
                      jnp.where(jj == cur, 2e9,
                                jnp.where(jj == cur - 1, 1e9, jnp.where(allowed, imp, NEG))))
    work = score
    thr = jnp.zeros((1, tq), F32)
    for _ in range(N_SLC):
        thr = jnp.max(work, axis=0, keepdims=True)
        work = jnp.where(work >= thr, -BIG, work)
    mask_ref[...] = jnp.where((score >= thr) & allowed, 0.0, NEG)


def _cmp_select(proj, kcp, vcpt, bnear, b31, tt):
    s = proj.shape[0]
    tq = CMP_TQ
    n_rows = kcp.shape[1]
    n_slc = tt.shape[0]
    return pl.pallas_call(
        functools.partial(_cmp_select_kernel, tq=tq, n_rows=n_rows),
        grid=(N_KV_B, s // tq),
        in_specs=[pl.BlockSpec(memory_space=pltpu.SMEM),
                  pl.BlockSpec((tq, Q_PER_KV_B * LANES), lambda g, n: (n, COL_QB // Q_PER_KV_B + g)),
                  pl.BlockSpec((None, n_rows, HEAD_DIM), lambda g, n: (g, 0, 0)),
                  pl.BlockSpec((None, HEAD_DIM, n_rows), lambda g, n: (g, 0, 0)),
                  pl.BlockSpec((None, Q_PER_KV_B, CMP_BAND, tq), lambda g, n: (g, 0, 0, 0)),
                  pl.BlockSpec((n_slc, n_rows), lambda g, n: (0, 0))],
        out_specs=[pl.BlockSpec((tq, Q_PER_KV_B * LANES), lambda g, n: (n, g)),
                   pl.BlockSpec((None, n_slc, tq), lambda g, n: (g, 0, n))],
        out_shape=[jax.ShapeDtypeStruct((s, N_HEADS_B * HEAD_DIM), BF16),
                   jax.ShapeDtypeStruct((N_KV_B, n_slc, s), F32)],
        scratch_shapes=[pltpu.VMEM((n_rows, tq), F32)],
        compiler_params=_cparams(("parallel", "arbitrary")),
        name="nsa_cmp_select",
    )(b31, proj, kcp, vcpt, bnear, tt)


def _slc_kernel(q_ref, k_ref, vt_ref, mask_ref, btile_ref, o_ref,
                s_scr, p_scr, alpha_scr, m_scr, l_scr, acc_scr, *, tq):
    n = pl.program_id(1)
    last = n
    blocks_per_chunk = tq // SLC_BLOCK
    q = jnp.concatenate([q_ref[:, r * LANES:(r + 1) * LANES] for r in range(Q_PER_KV_B)], axis=0)

    def scores(c, slot):
        pen = jnp.where(c > last, NEG, 0.0)
        c = jnp.minimum(c, last)
        k0 = pl.multiple_of(c * tq, tq)
        mrows = jnp.concatenate(
            [jnp.broadcast_to(mask_ref[pl.ds(c * blocks_per_chunk + b, 1), :] + pen, (SLC_BLOCK, tq))
             for b in range(blocks_per_chunk)], axis=0)
        mrows = jnp.concatenate([mrows] * Q_PER_KV_B, axis=1)
        s = lax.dot_general(k_ref[pl.ds(k0, tq), :], q, (((1,), (1,)), ((), ())), preferred_element_type=F32)
        s_scr[slot] = s + mrows + btile_ref[jnp.minimum(n - c, SLC_NEAR - 1)]

    def softmax_update(slot):
        s = s_scr[slot]
        m_old = m_scr[...]
        m_new = jnp.maximum(m_old, jnp.max(s, axis=0, keepdims=True))
        alpha = jnp.exp(m_old - m_new)
        p = jnp.exp(s - m_new)
        l_scr[...] = alpha * l_scr[...] + jnp.sum(p, axis=0, keepdims=True)
        m_scr[...] = m_new
        alpha_scr[slot] = alpha
        p_scr[slot] = p.astype(BF16)

    def values(c, slot):
        k0 = pl.multiple_of(jnp.clip(c, 0, last) * tq, tq)
        acc_scr[...] = alpha_scr[slot] * acc_scr[...] + jnp.dot(
            vt_ref[:, pl.ds(k0, tq)], p_scr[slot], preferred_element_type=F32)

    m_scr[...] = jnp.full(m_scr.shape, -BIG, F32)
    l_scr[...] = jnp.zeros(l_scr.shape, F32)
    acc_scr[...] = jnp.zeros(acc_scr.shape, F32)
    p_scr[1] = jnp.zeros(p_scr.shape[1:], BF16)
    alpha_scr[1] = jnp.ones(alpha_scr.shape[1:], F32)
    scores(0, 0)

    def body(i, carry):
        c = 2 * i
        scores(c + 1, 1)
        softmax_update(0)
        values(c - 1, 1)
        scores(c + 2, 0)
        softmax_update(1)
        values(c, 0)
        return carry

    n_pairs = (last + 2) // 2
    lax.fori_loop(0, n_pairs, body, 0)
    values(2 * n_pairs - 1, 1)
    o_t = acc_scr[...] / l_scr[...]
    for r in range(Q_PER_KV_B):
        o_ref[:, r * LANES:(r + 1) * LANES] = o_t[:, r * tq:(r + 1) * tq].T.astype(o_ref.dtype)


def _slc_attention(proj, vt, mask, btile):
    s = proj.shape[0]
    tq = CMP_TQ
    n_slc = mask.shape[1]
    wide = Q_PER_KV_B * tq
    return pl.pallas_call(
        functools.partial(_slc_kernel, tq=tq),
        grid=(N_KV_B, s // tq),
        in_specs=[pl.BlockSpec((tq, Q_PER_KV_B * LANES), lambda g, n: (n, COL_QB // Q_PER_KV_B + g)),
                  pl.BlockSpec((s, HEAD_DIM), lambda g, n: (0, COL_KSLC + g)),
                  pl.BlockSpec((None, HEAD_DIM, s), lambda g, n: (g, 0, 0)),
                  pl.BlockSpec((None, n_slc, tq), lambda g, n: (g, 0, n)),
                  pl.BlockSpec((None, SLC_NEAR, tq, wide), lambda g, n: (g, 0, 0, 0))],
        out_specs=pl.BlockSpec((tq, Q_PER_KV_B * LANES), lambda g, n: (n, g)),
        out_shape=jax.ShapeDtypeStruct((s, N_HEADS_B * HEAD_DIM), BF16),
        scratch_shapes=[pltpu.VMEM((2, tq, wide), F32),
                        pltpu.VMEM((2, tq, wide), BF16),
                        pltpu.VMEM((2, 1, wide), F32),
                        pltpu.VMEM((1, wide), F32),
                        pltpu.VMEM((1, wide), F32),
                        pltpu.VMEM((HEAD_DIM, wide), F32)],
        compiler_params=_cparams(("parallel", "arbitrary")),
        name="nsa_selected",
    )(proj, proj, vt, mask, btile)


def _combine_kernel(o1_ref, o2_ref, o3_ref, l1_ref, l2_ref, l3_ref, oc_ref, os_ref, ow_ref, gl_ref, g_ref, o_ref):
    heads_per_lse_block = 8
    for h in range(N_HEADS_A):
        sl = slice(h * LANES, (h + 1) * LANES)
        col = (h // heads_per_lse_block) * LANES + h % heads_per_lse_block
        l1 = l1_ref[:, col:col + 1]
        l2 = l2_ref[:, col:col + 1]
        l3 = l3_ref[:, col:col + 1]
        m = jnp.maximum(jnp.maximum(l1, l2), l3)
        e1, e2, e3 = jnp.exp(l1 - m), jnp.exp(l2 - m), jnp.exp(l3 - m)
        den = e1 + e2 + e3
        o = (o1_ref[:, sl].astype(F32) * (e1 / den) + o2_ref[:, sl].astype(F32) * (e2 / den)
             + o3_ref[:, sl].astype(F32) * (e3 / den))
        ms = jnp.mean(o * o, axis=-1, keepdims=True)
        o_ref[:, sl] = (o * lax.rsqrt(ms + RMS_EPS) * g_ref[h:h + 1, :]).astype(o_ref.dtype)
    gates = jax.nn.sigmoid(gl_ref[...])
    for h in range(N_HEADS_B):
        sl = slice(h * LANES, (h + 1) * LANES)
        o = (gates[:, 3 * h:3 * h + 1] * oc_ref[:, sl].astype(F32)
             + gates[:, 3 * h + 1:3 * h + 2] * os_ref[:, sl].astype(F32)
             + gates[:, 3 * h + 2:3 * h + 3] * ow_ref[:, sl].astype(F32))
        ms = jnp.mean(o * o, axis=-1, keepdims=True)
        hh = N_HEADS_A + h
        o_ref[:, hh * LANES:(hh + 1) * LANES] = (
            o * lax.rsqrt(ms + RMS_EPS) * g_ref[hh:hh + 1, :]).astype(o_ref.dtype)


def _combine(o_dil, lse_dil, o_cmp, o_slc, o_win, gl, out_gain, tm=256):
    s = o_cmp.shape[0]
    wa = N_HEADS_A * HEAD_DIM
    wide = pl.BlockSpec((tm, wa), lambda i: (i, 0))
    lse_spec = pl.BlockSpec((tm, 2 * LANES), lambda i: (i, 0))
    return pl.pallas_call(
        _combine_kernel,
        grid=(s // tm,),
        in_specs=[wide, wide, wide, lse_spec, lse_spec, lse_spec, wide, wide, wide,
                  pl.BlockSpec((tm, LANES), lambda i: (i, 0)),
                  pl.BlockSpec((N_HEADS_A + N_HEADS_B, HEAD_DIM), lambda i: (0, 0))],
        out_specs=pl.BlockSpec((tm, 2 * wa), lambda i: (i, 0)),
        out_shape=jax.ShapeDtypeStruct((s, 2 * wa), BF16),
        compiler_params=_cparams(("parallel",)),
        name="combine_headnorm",
    )(*o_dil, *lse_dil, o_cmp, o_slc, o_win, gl, out_gain)


def _topk_rounds(work, k):
    vals = []
    for _ in range(k):
        mx = jnp.max(work, axis=0, keepdims=True)
        vals.append(mx)
        work = jnp.where(work >= mx, -BIG, work)
    return vals


def _peer_topk_kernel(q_ref, keys_ref, s1_ref, s2_ref, rows_ref, *, tm):
    half = N_KEYS
    dn = (((1,), (1,)), ((), ()))
    s1 = lax.dot_general(keys_ref[0], q_ref[:, 0:half], dn, preferred_element_type=F32)
    s2 = lax.dot_general(keys_ref[1], q_ref[:, half:2 * half], dn, preferred_element_type=F32)
    v1 = _topk_rounds(s1, PEER_TOPK)
    v2 = _topk_rounds(s2, PEER_TOPK)
    v2m = jnp.concatenate(v2, axis=0)
    cand = jnp.concatenate([v1[a] + v2m for a in range(PEER_TOPK)], axis=0)
    top = _topk_rounds(cand, PEER_TOPK)
    z = jnp.zeros((1, tm), F32)
    for tv in top:
        z = z + jnp.exp(tv - top[0])
    s1_ref[...] = s1
    s2_ref[...] = s2
    rows_ref[...] = jnp.concatenate([top[PEER_TOPK - 1], v1[0], v2[0], 1.0 / z] * 2, axis=0)


ROW_THR, ROW_MAX1, ROW_MAX2, ROW_INVZ = 0, 1, 2, 3


def _peer_topk(qp, keys, tm=256):
    s = qp.shape[0]
    stat = jax.ShapeDtypeStruct((PEER_HEADS, N_KEYS, s), F32)
    stat_spec = pl.BlockSpec((None, N_KEYS, tm), lambda n, h: (h, 0, n))
    return pl.pallas_call(
        functools.partial(_peer_topk_kernel, tm=tm),
        grid=(s // tm, PEER_HEADS),
        in_specs=[pl.BlockSpec((tm, 2 * N_KEYS), lambda n, h: (n, h)),
                  pl.BlockSpec((None, 2, N_KEYS, N_KEYS), lambda n, h: (h, 0, 0, 0))],
        out_specs=[stat_spec, stat_spec, pl.BlockSpec((None, 8, tm), lambda n, h: (h, 0, n))],
        out_shape=[stat, stat, jax.ShapeDtypeStruct((PEER_HEADS, 8, s), F32)],
        compiler_params=_cparams(("parallel", "arbitrary")),
        name="peer_topk",
    )(qp, keys)


def _peer_dense_kernel(xn_ref, down_ref, up_ref, s1_ref, s2_ref, rows_ref, o_ref, *, tm, te):
    j = pl.program_id(1)

    @pl.when(j == 0)
    def _():
        o_ref[...] = jnp.zeros(o_ref.shape, F32)

    half = te // 2
    i1_per_half = half // N_KEYS
    n_i1 = 2 * i1_per_half
    dn = (((1,), (1,)), ((), ()))
    xn = xn_ref[...]
    hid = [lax.dot_general(down_ref[k * half:(k + 1) * half, :], xn, dn, preferred_element_type=F32)
           for k in range(2)]
    w = [jnp.zeros((N_KEYS, tm), F32) for _ in range(n_i1)]
    for h in range(PEER_HEADS):
        thr = rows_ref[h, ROW_THR:ROW_THR + 1, :]
        max1 = rows_ref[h, ROW_MAX1:ROW_MAX1 + 1, :]
        max2 = rows_ref[h, ROW_MAX2:ROW_MAX2 + 1, :]
        inv_z = rows_ref[h, ROW_INVZ:ROW_INVZ + 1, :]
        s2 = s2_ref[h]
        e2 = jnp.exp(s2 - max2)
        for ii in range(n_i1):
            s1row = s1_ref[h, pl.ds(j * n_i1 + ii, 1), :]
            e1row = jnp.exp(s1row - max1) * inv_z
            w[ii] = w[ii] + jnp.where(s1row + s2 >= thr, e2, 0.0) * e1row
    for k in range(2):
        act = jax.nn.gelu(hid[k])
        a_t = jnp.concatenate([w[k * i1_per_half + ii] * act[ii * N_KEYS:(ii + 1) * N_KEYS]
                               for ii in range(i1_per_half)], axis=0)
        o_ref[...] += jnp.dot(a_t.T.astype(BF16), up_ref[k * half:(k + 1) * half, :],
                              preferred_element_type=F32)


def _peer_dense(xn, down, up, s1, s2, rows, tm, te):
    s, d = xn.shape
    stat_spec = pl.BlockSpec((PEER_HEADS, N_KEYS, tm), lambda n, j: (0, 0, n))
    return pl.pallas_call(
        functools.partial(_peer_dense_kernel, tm=tm, te=te),
        grid=(s // tm, down.shape[0] // te),
        in_specs=[pl.BlockSpec((tm, d), lambda n, j: (n, 0)),
                  pl.BlockSpec((te, d), lambda n, j: (j, 0)),
                  pl.BlockSpec((te, d), lambda n, j: (j, 0)),
                  stat_spec, stat_spec,
                  pl.BlockSpec((PEER_HEADS, 8, tm), lambda n, j: (0, 0, n))],
        out_specs=pl.BlockSpec((tm, d), lambda n, j: (n, 0)),
        out_shape=jax.ShapeDtypeStruct((s, d), F32),
        compiler_params=_cparams(("parallel", "arbitrary")),
        name="peer_dense",
    )(xn, down, up, s1, s2, rows)


def _add_kernel(a_ref, b_ref, o_ref):
    o_ref[...] = a_ref[...] + b_ref[...]


def _add(a, b, tm=256):
    s, d = a.shape
    spec = pl.BlockSpec((tm, d), lambda i: (i, 0))
    return pl.pallas_call(
        _add_kernel, grid=(s // tm,), in_specs=[spec, spec], out_specs=spec,
        out_shape=jax.ShapeDtypeStruct((s, d), F32),
        compiler_params=_cparams(("parallel",)), name="residual_add",
    )(a, b)


def _tile(n, pref):
    t = pref
    while n % t:
        t //= 2
    return t


def _layer(x, attn_norm_g, w_in, qk_gain_a, qk_gain_b, rel_bias, cmp_pos, cmp_w1, cmp_w2,
           out_norm_g, w_out, ffn_norm_g, peer_w_query, peer_sub_keys, peer_down, peer_up):
    s, d = x.shape
    scale = HEAD_DIM ** -0.5
    ones = jnp.ones((HEAD_DIM,), F32)
    zeros = jnp.zeros((HEAD_DIM,), F32)

    xn = _rmsnorm(x, attn_norm_g)
    head_gain = ([qk_gain_a[0] * scale] * 16 + [qk_gain_a[1]] * 16 + [ones] * 16 + [qk_gain_b[0] * scale] * 16
                 + [ones] * 8 + [qk_gain_b[2]] * 4 + [ones] * 4 + [qk_gain_b[3]] * 4 + [ones] * 4)
    head_mode = ([ones] * 32 + [zeros] * 16 + [ones] * 16 + [zeros] * 8 + [ones] * 4 + [zeros] * 4
                 + [ones] * 4 + [zeros] * 4)
    gain = jnp.concatenate(head_gain).astype(F32).reshape(1, PROJ_COLS)
    mode = jnp.concatenate(head_mode).reshape(1, PROJ_COLS)
    w_main = w_in[:, :PROJ_COLS].astype(BF16)
    n_gate = w_in.shape[1] - PROJ_COLS
    w_gate = jnp.pad(w_in[:, PROJ_COLS:], ((0, 0), (0, LANES - n_gate))).astype(BF16)
    tm = _tile(s, 1024)
    proj = _proj_headnorm(xn, w_main, gain, mode, tm, 1024)
    gl = _matmul(xn, w_gate, F32, tm, LANES)

    table_a = rel_bias[:N_HEADS_A]
    table_b = rel_bias[N_HEADS_A:]

    o_dil, lse_dil = [], []
    hb = 8
    for window, dil in DILATED_PATTERNS:
        length = s // dil
        view = proj.reshape(length, dil * PROJ_COLS)
        bias = _band_bias(table_a, window // dil, dil, 1).reshape(N_HEADS_A // hb, hb, BAND_TQ, 2 * BAND_TQ)
        blocks = PROJ_COLS // (hb * LANES)
        o, lse = _banded_attention(
            view, bias, n_r=dil, n_hb=N_HEADS_A // hb, hb=hb, r=1, nprev=1,
            qcol=lambda rr, hh, _b=blocks: rr * _b + COL_QA // hb + hh,
            kcol=lambda rr, hh, _b=blocks: rr * _b + COL_KA // hb + hh,
            vcol=lambda rr, hh, _b=blocks: rr * _b + COL_VA // hb + hh,
            with_lse=True)
        o_dil.append(o.reshape(s, N_HEADS_A * HEAD_DIM))
        lse_dil.append(lse.reshape(s, (N_HEADS_A // hb) * LANES))

    nprev_w = -(-(WIN_B - 1) // BAND_TQ)
    bias_w = _band_bias(table_b, WIN_B - 1, 1, nprev_w).reshape(N_KV_B, Q_PER_KV_B, BAND_TQ, (nprev_w + 1) * BAND_TQ)
    o_win = _banded_attention(
        proj, bias_w, n_r=1, n_hb=N_KV_B, hb=Q_PER_KV_B, r=Q_PER_KV_B, nprev=nprev_w,
        qcol=lambda rr, hh: COL_QB // Q_PER_KV_B + hh,
        kcol=lambda rr, hh: COL_KWIN + hh,
        vcol=lambda rr, hh: COL_VWIN + hh,
        with_lse=False)[0]

    n_chunk = s // CMP_STRIDE
    n_cmp = (s - CMP_BLOCK) // CMP_STRIDE + 1
    raw = proj[:, COL_KCMP * HEAD_DIM:(COL_VCMP + N_KV_B) * HEAD_DIM]
    xt = raw.reshape(n_chunk, CMP_STRIDE, 2, N_KV_B, HEAD_DIM).transpose(2, 3, 1, 0, 4)
    kv_c = _compress(xt, cmp_w1.astype(BF16), cmp_w2.astype(BF16),
                     cmp_pos.reshape(2, 1, CMP_BLOCK * HEAD_DIM).astype(BF16),
                     qk_gain_b[1].reshape(1, HEAD_DIM).astype(F32))
    n_rows = CMP_PAD + n_chunk + 16
    kv_c = jnp.pad(kv_c[:, :, :n_cmp], ((0, 0), (0, 0), (CMP_PAD, n_rows - CMP_PAD - n_cmp), (0, 0)))
    kcp = kv_c[0]
    vcpt = kv_c[1].transpose(0, 2, 1)

    tl = np.arange(CMP_TQ)[None, :]
    il = np.arange(CMP_BAND)[:, None]
    rel_near = CMP_STRIDE * CMP_PAD - (CMP_BLOCK - 1) + tl - CMP_STRIDE * il
    bnear = _bias_of_rel(table_b, rel_near).reshape(N_KV_B, Q_PER_KV_B, CMP_BAND, CMP_TQ)
    b31 = table_b[:, N_BUCKETS - 1].astype(F32)
    n_slc = s // SLC_BLOCK
    ratio = SLC_BLOCK // CMP_STRIDE
    tt_np = np.zeros((n_slc, n_rows), np.float32)
    for jblk in range(n_slc):
        for off, wgt in ((-1, 1.0), (0, 2.0), (1, 2.0), (2, 2.0), (3, 1.0)):
            i_c = ratio * jblk + off
            if 0 <= i_c < n_cmp:
                tt_np[jblk, CMP_PAD + i_c] = wgt
    tt = jnp.asarray(tt_np, BF16)
    o_cmp, mask = _cmp_select(proj, kcp, vcpt, bnear, b31, tt)

    vt = proj[:, COL_VSLC * HEAD_DIM:(COL_VSLC + N_KV_B) * HEAD_DIM].reshape(s, N_KV_B, HEAD_DIM).transpose(1, 2, 0)
    kl = np.arange(CMP_TQ)[:, None]
    od = np.arange(SLC_NEAR)[:, None, None]
    rel_s = CMP_TQ * od + tl[None] - kl[None]
    btile = _bias_of_rel(table_b, rel_s).reshape(N_KV_B, Q_PER_KV_B, SLC_NEAR, CMP_TQ, CMP_TQ)
    btile = btile.transpose(0, 2, 3, 1, 4).reshape(N_KV_B, SLC_NEAR, CMP_TQ, Q_PER_KV_B * CMP_TQ)
    o_slc = _slc_attention(proj, vt, mask, btile)

    o_n = _combine(o_dil, lse_dil, o_cmp, o_slc, o_win, gl, out_norm_g.astype(F32))
    h = _matmul_residual(o_n, w_out.astype(BF16), x, tm, _tile(d, 1024))

    xn2 = _rmsnorm(h, ffn_norm_g)
    qp = _matmul(xn2, peer_w_query.astype(BF16), BF16, tm, 1024)
    s1, s2, rows = _peer_topk(qp, peer_sub_keys.astype(BF16))
    peer = _peer_dense(xn2, peer_down.astype(BF16), peer_up.astype(BF16), s1, s2, rows, _tile(s, 512), 512)
    return _add(h, peer)


def kernel(x, attn_norm_g, w_in, qk_gain_a, qk_gain_b, rel_bias, cmp_pos, cmp_w1, cmp_w2, out_norm_g, w_out,
           ffn_norm_g, peer_w_query, peer_sub_keys, peer_down, peer_up):
    b = x.shape[0]
    outs = [_layer(x[bi], attn_norm_g[0], w_in[0], qk_gain_a[0], qk_gain_b[0], rel_bias, cmp_pos[0], cmp_w1[0],
                   cmp_w2[0], out_norm_g[0], w_out[0], ffn_norm_g[0], peer_w_query[0], peer_sub_keys[0],
                   peer_down[0], peer_up[0]) for bi in range(b)]
    return jnp.stack(outs, axis=0)
```

```python
import functools
import math

import jax
import jax.numpy as jnp
import numpy as np
from jax import lax
from jax.experimental import pallas as pl
from jax.experimental.pallas import tpu as pltpu

F32 = jnp.float32
BF16 = jnp.bfloat16

HEAD_DIM = 128
LANES = 128
N_HEADS_A = 16
N_HEADS_B = 16
N_KV_B = 4
Q_PER_KV_B = 4
DILATED_PATTERNS = ((128, 1), (512, 4), (2048, 16))
CMP_BLOCK = 32
CMP_STRIDE = 16
SLC_BLOCK = 64
N_SLC = 16
WIN_B = 512
N_BUCKETS = 32
MAX_DISTANCE = 2048
PEER_HEADS = 8
N_KEYS = 128
PEER_TOPK = 16
RMS_EPS = 1e-6
NEG = -1e30
BIG = 3e38
LOG2E = math.log2(math.e)

COL_QA, COL_KA, COL_VA = 0, 16, 32
N_PROJ_A_HEADS = 48
COL_QB, COL_KCMP, COL_VCMP, COL_KSLC, COL_VSLC, COL_KWIN, COL_VWIN = 0, 16, 20, 24, 28, 32, 36
N_PROJ_B_HEADS = 40
PROJ_A_COLS = N_PROJ_A_HEADS * HEAD_DIM
PROJ_COLS = (N_PROJ_A_HEADS + N_PROJ_B_HEADS) * HEAD_DIM

BAND_TQ = 128
CMP_TQ = 256
CMP_PAD = 112
CMP_BAND = 128
SLC_NEAR = 8
VMEM_LIMIT = 56 * 1024 * 1024


def _cparams(sem):
    return pltpu.CompilerParams(dimension_semantics=sem, vmem_limit_bytes=VMEM_LIMIT)


def _rel_bucket(dist):
    n = np.maximum(dist, 0)
    max_exact = N_BUCKETS // 2
    nf = np.maximum(n, 1).astype(np.float32)
    log_part = (np.log(nf / np.float32(max_exact)) / np.float32(math.log(MAX_DISTANCE / max_exact))
                * np.float32(N_BUCKETS - max_exact))
    large = np.minimum(max_exact + log_part.astype(np.int32), N_BUCKETS - 1)
    return np.where(n < max_exact, n, large).astype(np.int32)


def _bias_of_rel(table, rel):
    rel = np.asarray(rel)
    bucket = jnp.asarray(np.where(rel >= 0, _rel_bucket(rel), -1))[None]
    table = table.astype(F32).reshape((table.shape[0], N_BUCKETS) + (1,) * rel.ndim)
    out = jnp.full((table.shape[0],) + rel.shape, NEG, F32)
    for k in range(N_BUCKETS):
        out = jnp.where(bucket == k, table[:, k], out)
    return out


def _rmsnorm_kernel(x_ref, g_ref, o_ref):
    x = x_ref[...]
    ms = jnp.mean(x * x, axis=-1, keepdims=True)
    o_ref[...] = (x * lax.rsqrt(ms + RMS_EPS) * g_ref[...]).astype(o_ref.dtype)


def _rmsnorm(x, g, tm=256):
    s, d = x.shape
    return pl.pallas_call(
        _rmsnorm_kernel,
        grid=(s // tm,),
        in_specs=[pl.BlockSpec((tm, d), lambda i: (i, 0)), pl.BlockSpec((1, d), lambda i: (0, 0))],
        out_specs=pl.BlockSpec((tm, d), lambda i: (i, 0)),
        out_shape=jax.ShapeDtypeStruct((s, d), BF16),
        compiler_params=_cparams(("parallel",)),
        name="rmsnorm",
    )(x, g.reshape(1, d).astype(F32))


def _proj_kernel(x_ref, w_ref, gain_ref, mode_ref, *refs, n_chunks, dilations, tm):
    outs = refs[:len(dilations)]
    acc = jnp.dot(x_ref[...], w_ref[...], preferred_element_type=F32)
    for c in range(n_chunks):
        sl = slice(c * LANES, (c + 1) * LANES)
        t = acc[:, sl]
        ms = jnp.mean(t * t, axis=-1, keepdims=True)
        mult = jnp.where(mode_ref[:, sl] > 0, lax.rsqrt(ms + RMS_EPS) * gain_ref[:, sl], 1.0)
        y = t * mult
        if len(dilations) > 1:
            y_scr = refs[len(dilations)]
            y_scr[c] = y
        for o_ref, dil in zip(outs, dilations):
            if dil == 1:
                o_ref[:, sl] = y.astype(o_ref.dtype)
            else:
                for r in range(dil):
                    o_ref[r, :, sl] = y_scr[c, pl.ds(r, tm // dil, stride=dil), :].astype(o_ref.dtype)


def _proj_headnorm(xn, w, gain, mode, tm, tn, dilations=(1,)):
    s, d = xn.shape
    n = w.shape[1]
    out_specs, out_shape = [], []
    for dil in dilations:
        if dil == 1:
            out_specs.append(pl.BlockSpec((tm, tn), lambda i, j: (i, j)))
            out_shape.append(jax.ShapeDtypeStruct((s, n), BF16))
        else:
            out_specs.append(pl.BlockSpec((dil, tm // dil, tn), lambda i, j: (0, i, j)))
            out_shape.append(jax.ShapeDtypeStruct((dil, s // dil, n), BF16))
    scratch = [pltpu.VMEM((tn // LANES, tm, LANES), F32)] if len(dilations) > 1 else []
    return pl.pallas_call(
        functools.partial(_proj_kernel, n_chunks=tn // LANES, dilations=tuple(dilations), tm=tm),
        grid=(s // tm, n // tn),
        in_specs=[pl.BlockSpec((tm, d), lambda i, j: (i, 0)),
                  pl.BlockSpec((d, tn), lambda i, j: (0, j)),
                  pl.BlockSpec((1, tn), lambda i, j: (0, j)),
                  pl.BlockSpec((1, tn), lambda i, j: (0, j))],
        out_specs=out_specs,
        out_shape=out_shape,
        scratch_shapes=scratch,
        compiler_params=_cparams(("parallel", "arbitrary")),
        name="proj_headnorm",
    )(xn, w, gain, mode)


def _matmul_kernel(x_ref, w_ref, o_ref):
    o_ref[...] = jnp.dot(x_ref[...], w_ref[...], preferred_element_type=F32).astype(o_ref.dtype)


def _matmul(x, w, out_dtype, tm, tn):
    s, d = x.shape
    n = w.shape[1]
    return pl.pallas_call(
        _matmul_kernel,
        grid=(s // tm, n // tn),
        in_specs=[pl.BlockSpec((tm, d), lambda i, j: (i, 0)), pl.BlockSpec((d, tn), lambda i, j: (0, j))],
        out_specs=pl.BlockSpec((tm, tn), lambda i, j: (i, j)),
        out_shape=jax.ShapeDtypeStruct((s, n), out_dtype),
        compiler_params=_cparams(("parallel", "arbitrary")),
        name="matmul",
    )(x, w)


def _matmul_res_kernel(x_ref, w_ref, r_ref, o_ref):
    o_ref[...] = r_ref[...] + jnp.dot(x_ref[...], w_ref[...], preferred_element_type=F32)


def _matmul_residual(x, w, res, tm, tn):
    s, d = x.shape
    n = w.shape[1]
    return pl.pallas_call(
        _matmul_res_kernel,
        grid=(s // tm, n // tn),
        in_specs=[pl.BlockSpec((tm, d), lambda i, j: (i, 0)),
                  pl.BlockSpec((d, tn), lambda i, j: (0, j)),
                  pl.BlockSpec((tm, tn), lambda i, j: (i, j))],
        out_specs=pl.BlockSpec((tm, tn), lambda i, j: (i, j)),
        out_shape=jax.ShapeDtypeStruct((s, n), F32),
        compiler_params=_cparams(("parallel", "arbitrary")),
        name="matmul_residual",
    )(x, w, res)


def _banded_kernel(*refs, hb, r, nprev, tq, with_lse):
    nk = nprev + 1
    q_ref = refs[0]
    k_refs = refs[1:1 + nk]
    v_refs = refs[1 + nk:1 + 2 * nk]
    bias_ref = refs[1 + 2 * nk]
    o_ref = refs[2 + 2 * nk]
    i = pl.program_id(2)
    span = nk * tq
    ng = hb // r
    q3 = jnp.stack([jnp.concatenate([q_ref[:, (g * r + rr) * LANES:(g * r + rr + 1) * LANES] for rr in range(r)],
                                    axis=0) for g in range(ng)], axis=0)
    parts = []
    for j in range(nk):
        k3 = jnp.stack([k_refs[j][:, g * LANES:(g + 1) * LANES] for g in range(ng)], axis=0)
        sj = jnp.einsum('gqd,gkd->gqk', q3, k3, preferred_element_type=F32)
        if j < nprev:
            sj = sj + jnp.where(i >= nprev - j, 0.0, NEG)
        parts.append(sj)
    s = jnp.concatenate(parts, axis=2) + bias_ref[...].reshape(ng, r * tq, span)
    m = jnp.max(s, axis=-1, keepdims=True)
    p = jnp.exp2(s - m)
    den = jnp.sum(p, axis=-1, keepdims=True)
    pb = p.astype(BF16)
    o = None
    for j in range(nk):
        v3 = jnp.stack([v_refs[j][:, g * LANES:(g + 1) * LANES] for g in range(ng)], axis=0)
        oj = jnp.einsum('gqk,gkd->gqd', pb[:, :, j * tq:(j + 1) * tq], v3, preferred_element_type=F32)
        o = oj if o is None else o + oj
    o = o / den
    for g in range(ng):
        for rr in range(r):
            h = g * r + rr
            o_ref[:, h * LANES:(h + 1) * LANES] = o[g, rr * tq:(rr + 1) * tq].astype(o_ref.dtype)
    if with_lse:
        lse_ref = refs[3 + 2 * nk]
        lse = m + jnp.log(den) * LOG2E
        lane = lax.broadcasted_iota(jnp.int32, (tq, LANES), 1)
        lse_mat = jnp.zeros((tq, LANES), F32)
        for g in range(ng):
            lse_mat = jnp.where(lane == g, lse[g], lse_mat)
        lse_ref[...] = lse_mat


def _banded_attention(arr, bias, *, n_hb, hb, r, nprev, qcol, kcol, vcol, with_lse):
    tq = BAND_TQ
    n_r, length = arr.shape[0], arr.shape[1]
    nk = nprev + 1
    qw = hb * LANES
    kw = (hb // r) * LANES

    def kmap(col, back):
        return lambda rr, hh, i: (rr, jnp.maximum(i - back, 0), col(hh))

    in_specs = [pl.BlockSpec((None, tq, qw), lambda rr, hh, i: (rr, i, qcol(hh)))]
    in_specs += [pl.BlockSpec((None, tq, kw), kmap(kcol, nprev - j)) for j in range(nk)]
    in_specs += [pl.BlockSpec((None, tq, kw), kmap(vcol, nprev - j)) for j in range(nk)]
    in_specs += [pl.BlockSpec((None, hb, tq, nk * tq), lambda rr, hh, i: (hh, 0, 0, 0))]
    out_specs = [pl.BlockSpec((tq, qw), lambda rr, hh, i: (i, rr * n_hb + hh))]
    out_shape = [jax.ShapeDtypeStruct((length, n_r * n_hb * qw), BF16)]
    if with_lse:
        out_specs.append(pl.BlockSpec((tq, LANES), lambda rr, hh, i: (i, rr * n_hb + hh)))
        out_shape.append(jax.ShapeDtypeStruct((length, n_r * n_hb * LANES), F32))
    return pl.pallas_call(
        functools.partial(_banded_kernel, hb=hb, r=r, nprev=nprev, tq=tq, with_lse=with_lse),
        grid=(n_r, n_hb, length // tq),
        in_specs=in_specs,
        out_specs=out_specs,
        out_shape=out_shape,
        compiler_params=_cparams(("parallel", "parallel", "arbitrary")),
        name="banded_attention",
    )(*([arr] * (1 + 2 * nk)), bias)


def _band_bias(table, max_dist, dist_scale, nprev):
    tq = BAND_TQ
    q_loc = np.arange(tq)[:, None]
    k_loc = np.arange((nprev + 1) * tq)[None, :] - nprev * tq
    rel = q_loc - k_loc
    return _bias_of_rel(table, np.where((rel >= 0) & (rel <= max_dist), rel * dist_scale, -1))


def _compress_kernel(x_ref, w1_ref, w2_ref, pos_ref, gain_ref, o_ref, *, n_chunk):
    half = CMP_STRIDE
    a = jnp.zeros((n_chunk, HEAD_DIM), F32)
    b = jnp.zeros((n_chunk, HEAD_DIM), F32)
    for c in range(half):
        xc = x_ref[c]
        a = a + jnp.dot(xc, w1_ref[c * HEAD_DIM:(c + 1) * HEAD_DIM, :], preferred_element_type=F32)
        b = b + jnp.dot(xc, w1_ref[(half + c) * HEAD_DIM:(half + c + 1) * HEAD_DIM, :],
                        preferred_element_type=F32)
    pos = jnp.broadcast_to(pos_ref[...], (8, CMP_BLOCK * HEAD_DIM))
    posterm = jnp.dot(pos, w1_ref[...], preferred_element_type=F32)[0:1, :]
    pre = a + pltpu.roll(b, n_chunk - 1, 0) + posterm
    hid = jax.nn.gelu(pre)
    out = jnp.dot(hid.astype(BF16), w2_ref[...], preferred_element_type=F32)
    ms = jnp.mean(out * out, axis=-1, keepdims=True)
    normed = out * lax.rsqrt(ms + RMS_EPS) * gain_ref[...]
    o_ref[...] = jnp.where(pl.program_id(0) == 0, normed, out).astype(o_ref.dtype)


def _compress(xt, w1, w2, pos, gain):
    n_chunk = xt.shape[3]
    return pl.pallas_call(
        functools.partial(_compress_kernel, n_chunk=n_chunk),
        grid=(2, N_KV_B),
        in_specs=[pl.BlockSpec((None, None, CMP_STRIDE, n_chunk, HEAD_DIM), lambda w, g: (w, g, 0, 0, 0)),
                  pl.BlockSpec((None, CMP_BLOCK * HEAD_DIM, HEAD_DIM), lambda w, g: (w, 0, 0)),
                  pl.BlockSpec((None, HEAD_DIM, HEAD_DIM), lambda w, g: (w, 0, 0)),
                  pl.BlockSpec((None, 1, CMP_BLOCK * HEAD_DIM), lambda w, g: (w, 0, 0)),
                  pl.BlockSpec((1, HEAD_DIM), lambda w, g: (0, 0))],
        out_specs=pl.BlockSpec((None, None, n_chunk, HEAD_DIM), lambda w, g: (w, g, 0, 0)),
        out_shape=jax.ShapeDtypeStruct((2, N_KV_B, n_chunk, HEAD_DIM), BF16),
        compiler_params=_cparams(("parallel", "parallel")),
        name="nsa_compress",
    )(xt, w1, w2, pos, gain)


def _cmp_select_kernel(b31_ref, q_ref, kc_ref, vct_ref, bnear_ref, tt_ref, o_ref, mask_ref, p_scr, *, tq, n_rows):
    g = pl.program_id(0)
    n = pl.program_id(1)
    t0 = n * tq
    band_lo = pl.multiple_of(n * (tq // CMP_STRIDE), 16)
    rows = lax.broadcasted_iota(jnp.int32, (n_rows, tq), 0)
    far_ok = (rows >= CMP_PAD) & (rows < band_lo)
    brow = lax.broadcasted_iota(jnp.int32, (CMP_BAND, tq), 0) + band_lo
    band_pen = jnp.where(brow >= CMP_PAD, 0.0, NEG)
    t = t0 + lax.broadcasted_iota(jnp.int32, (1, tq), 1)
    valid = t >= CMP_BLOCK - 1
    kc_all = kc_ref[...]
    kc_band = kc_ref[pl.ds(band_lo, CMP_BAND), :]
    psum = jnp.zeros((n_rows, tq), F32)
    for r in range(Q_PER_KV_B):
        q = q_ref[:, r * LANES:(r + 1) * LANES]
        b31 = b31_ref[g * Q_PER_KV_B + r]
        s_far = lax.dot_general(kc_all, q, (((1,), (1,)), ((), ())), preferred_element_type=F32)
        s_far = jnp.where(far_ok, s_far + b31, NEG)
        s_band = lax.dot_general(kc_band, q, (((1,), (1,)), ((), ())), preferred_element_type=F32)
        s_band = s_band + bnear_ref[r] + band_pen
        m = jnp.maximum(jnp.max(s_far, axis=0, keepdims=True), jnp.max(s_band, axis=0, keepdims=True))
        p_far = jnp.exp2(s_far - m)
        p_band = jnp.exp2(s_band - m)
        den = jnp.sum(p_far, axis=0, keepdims=True) + jnp.sum(p_band, axis=0, keepdims=True)
        inv = jnp.where(valid, 1.0 / den, 0.0)
        p_scr[...] = p_far * inv
        p_scr[pl.ds(band_lo, CMP_BAND), :] = p_band * inv
        p = p_scr[...]
        o_t = jnp.dot(vct_ref[...], p.astype(BF16), preferred_element_type=F32)
        o_ref[:, r * LANES:(r + 1) * LANES] = o_t.T.astype(o_ref.dtype)
        psum = psum + p
    p_hi = psum.astype(BF16)
    p_lo = (psum - p_hi.astype(F32)).astype(BF16)
    imp = (jnp.dot(tt_ref[...], p_hi, preferred_element_type=F32)
           + jnp.dot(tt_ref[...], p_lo, preferred_element_type=F32))
    n_slc = imp.shape[0]
    jj = lax.broadcasted_iota(jnp.int32, (n_slc, tq), 0)
    cur = t // SLC_BLOCK
    allowed = jj <= cur
    score = jnp.where(jj == 0, 3e9,
                      jnp.where(jj == cur, 2e9,
                                jnp.where(jj == cur - 1, 1e9, jnp.where(allowed, imp, NEG))))
    work = score
    thr = jnp.zeros((1, tq), F32)
    for _ in range(N_SLC):
        thr = jnp.max(work, axis=0, keepdims=True)
        work = jnp.where(work >= thr, -BIG, work)
    mask_ref[...] = jnp.where((score >= thr) & allowed, 0.0, NEG)


def _cmp_select(proj, kcp, vcpt, bnear, b31, tt):
    s = proj.shape[0]
    tq = CMP_TQ
    n_rows = kcp.shape[1]
    n_slc = tt.shape[0]
    return pl.pallas_call(
        functools.partial(_cmp_select_kernel, tq=tq, n_rows=n_rows),
        grid=(N_KV_B, s // tq),
        in_specs=[pl.BlockSpec(memory_space=pltpu.SMEM),
                  pl.BlockSpec((tq, Q_PER_KV_B * LANES), lambda g, n: (n, COL_QB // Q_PER_KV_B + g)),
                  pl.BlockSpec((None, n_rows, HEAD_DIM), lambda g, n: (g, 0, 0)),
                  pl.BlockSpec((None, HEAD_DIM, n_rows), lambda g, n: (g, 0, 0)),
                  pl.BlockSpec((None, Q_PER_KV_B, CMP_BAND, tq), lambda g, n: (g, 0, 0, 0)),
                  pl.BlockSpec((n_slc, n_rows), lambda g, n: (0, 0))],
        out_specs=[pl.BlockSpec((tq, Q_PER_KV_B * LANES), lambda g, n: (n, g)),
                   pl.BlockSpec((None, n_slc, tq), lambda g, n: (g, 0, n))],
        out_shape=[jax.ShapeDtypeStruct((s, N_HEADS_B * HEAD_DIM), BF16),
                   jax.ShapeDtypeStruct((N_KV_B, n_slc, s), F32)],
        scratch_shapes=[pltpu.VMEM((n_rows, tq), F32)],
        compiler_params=_cparams(("parallel", "arbitrary")),
        name="nsa_cmp_select",
    )(b31, proj, kcp, vcpt, bnear, tt)


def _slc_kernel(q_ref, k_ref, vt_ref, mask_ref, btile_ref, o_ref,
                s_scr, p_scr, alpha_scr, m_scr, acc_scr, *, tq):
    n = pl.program_id(1)
    last = n
    blocks_per_chunk = tq // SLC_BLOCK
    q = jnp.concatenate([q_ref[:, r * LANES:(r + 1) * LANES] for r in range(Q_PER_KV_B)], axis=0)

    def scores(c, slot):
        pen = jnp.where(c > last, NEG, 0.0)
        c = jnp.minimum(c, last)
        k0 = pl.multiple_of(c * tq, tq)
        mrows = jnp.concatenate(
            [jnp.broadcast_to(mask_ref[pl.ds(c * blocks_per_chunk + b, 1), :] + pen, (SLC_BLOCK, tq))
             for b in range(blocks_per_chunk)], axis=0)
        mrows = jnp.concatenate([mrows] * Q_PER_KV_B, axis=1)
        s = lax.dot_general(k_ref[pl.ds(k0, tq), :], q, (((1,), (1,)), ((), ())), preferred_element_type=F32)
        s_scr[slot] = s + mrows + btile_ref[jnp.minimum(n - c, SLC_NEAR - 1)]

    def softmax_update(slot):
        s = s_scr[slot]
        m_old = m_scr[...]
        m_new = jnp.maximum(m_old, jnp.max(s, axis=0, keepdims=True))
        alpha = jnp.exp2(m_old - m_new)
        p = jnp.exp2(s - m_new)
        m_scr[...] = m_new
        alpha_scr[slot] = alpha
        p_scr[slot] = p.astype(BF16)

    def values(c, slot):
        k0 = pl.multiple_of(jnp.clip(c, 0, last) * tq, tq)
        acc_scr[...] = alpha_scr[slot] * acc_scr[...] + jnp.dot(
            vt_ref[:, pl.ds(k0, tq)], p_scr[slot], preferred_element_type=F32)

    m_scr[...] = jnp.full(m_scr.shape, -BIG, F32)
    acc_scr[...] = jnp.zeros(acc_scr.shape, F32)
    p_scr[1] = jnp.zeros(p_scr.shape[1:], BF16)
    alpha_scr[1] = jnp.ones(alpha_scr.shape[1:], F32)
    scores(0, 0)

    def body(i, carry):
        c = 2 * i
        scores(c + 1, 1)
        softmax_update(0)
        values(c - 1, 1)
        scores(c + 2, 0)
        softmax_update(1)
        values(c, 0)
        return carry

    n_pairs = (last + 2) // 2
    lax.fori_loop(0, n_pairs, body, 0)
    values(2 * n_pairs - 1, 1)
    o_t = acc_scr[0:HEAD_DIM, :] / acc_scr[HEAD_DIM:HEAD_DIM + 1, :]
    for r in range(Q_PER_KV_B):
        o_ref[:, r * LANES:(r + 1) * LANES] = o_t[:, r * tq:(r + 1) * tq].T.astype(o_ref.dtype)


def _slc_attention(proj, vt, mask, btile):
    s = proj.shape[0]
    tq = CMP_TQ
    n_slc = mask.shape[1]
    wide = Q_PER_KV_B * tq
    v_rows = vt.shape[1]
    return pl.pallas_call(
        functools.partial(_slc_kernel, tq=tq),
        grid=(N_KV_B, s // tq),
        in_specs=[pl.BlockSpec((tq, Q_PER_KV_B * LANES), lambda g, n: (n, COL_QB // Q_PER_KV_B + g)),
                  pl.BlockSpec((s, HEAD_DIM), lambda g, n: (0, COL_KSLC + g)),
                  pl.BlockSpec((None, v_rows, s), lambda g, n: (g, 0, 0)),
                  pl.BlockSpec((None, n_slc, tq), lambda g, n: (g, 0, n)),
                  pl.BlockSpec((None, SLC_NEAR, tq, wide), lambda g, n: (g, 0, 0, 0))],
        out_specs=pl.BlockSpec((tq, Q_PER_KV_B * LANES), lambda g, n: (n, g)),
        out_shape=jax.ShapeDtypeStruct((s, N_HEADS_B * HEAD_DIM), BF16),
        scratch_shapes=[pltpu.VMEM((2, tq, wide), F32),
                        pltpu.VMEM((2, tq, wide), BF16),
                        pltpu.VMEM((2, 1, wide), F32),
                        pltpu.VMEM((1, wide), F32),
                        pltpu.VMEM((v_rows, wide), F32)],
        compiler_params=_cparams(("parallel", "arbitrary")),
        name="nsa_selected",
    )(proj, proj, vt, mask, btile)


def _combine_kernel(o1_ref, o2_ref, o3_ref, l1_ref, l2_ref, l3_ref, oc_ref, os_ref, ow_ref, gl_ref, g_ref, o_ref):
    heads_per_lse_block = 8
    for h in range(N_HEADS_A):
        sl = slice(h * LANES, (h + 1) * LANES)
        col = (h // heads_per_lse_block) * LANES + h % heads_per_lse_block
        l1 = l1_ref[:, col:col + 1]
        l2 = l2_ref[:, col:col + 1]
        l3 = l3_ref[:, col:col + 1]
        m = jnp.maximum(jnp.maximum(l1, l2), l3)
        e1, e2, e3 = jnp.exp2(l1 - m), jnp.exp2(l2 - m), jnp.exp2(l3 - m)
        den = e1 + e2 + e3
        o = (o1_ref[:, sl].astype(F32) * (e1 / den) + o2_ref[:, sl].astype(F32) * (e2 / den)
             + o3_ref[:, sl].astype(F32) * (e3 / den))
        ms = jnp.mean(o * o, axis=-1, keepdims=True)
        o_ref[:, sl] = (o * lax.rsqrt(ms + RMS_EPS) * g_ref[h:h + 1, :]).astype(o_ref.dtype)
    gates = jax.nn.sigmoid(gl_ref[...])
    for h in range(N_HEADS_B):
        sl = slice(h * LANES, (h + 1) * LANES)
        o = (gates[:, 3 * h:3 * h + 1] * oc_ref[:, sl].astype(F32)
             + gates[:, 3 * h + 1:3 * h + 2] * os_ref[:, sl].astype(F32)
             + gates[:, 3 * h + 2:3 * h + 3] * ow_ref[:, sl].astype(F32))
        ms = jnp.mean(o * o, axis=-1, keepdims=True)
        hh = N_HEADS_A + h
        o_ref[:, hh * LANES:(hh + 1) * LANES] = (
            o * lax.rsqrt(ms + RMS_EPS) * g_ref[hh:hh + 1, :]).astype(o_ref.dtype)


def _combine(o_dil, lse_dil, o_cmp, o_slc, o_win, gl, out_gain, tm=256):
    s = o_cmp.shape[0]
    wa = N_HEADS_A * HEAD_DIM
    wide = pl.BlockSpec((tm, wa), lambda i: (i, 0))
    lse_spec = pl.BlockSpec((tm, 2 * LANES), lambda i: (i, 0))
    return pl.pallas_call(
        _combine_kernel,
        grid=(s // tm,),
        in_specs=[wide, wide, wide, lse_spec, lse_spec, lse_spec, wide, wide, wide,
                  pl.BlockSpec((tm, LANES), lambda i: (i, 0)),
                  pl.BlockSpec((N_HEADS_A + N_HEADS_B, HEAD_DIM), lambda i: (0, 0))],
        out_specs=pl.BlockSpec((tm, 2 * wa), lambda i: (i, 0)),
        out_shape=jax.ShapeDtypeStruct((s, 2 * wa), BF16),
        compiler_params=_cparams(("parallel",)),
        name="combine_headnorm",
    )(*o_dil, *lse_dil, o_cmp, o_slc, o_win, gl, out_gain)


def _topk_rounds(work, k):
    vals = []
    for _ in range(k):
        mx = jnp.max(work, axis=0, keepdims=True)
        vals.append(mx)
        work = jnp.where(work >= mx, -BIG, work)
    return vals


def _peer_topk_kernel(q_ref, keys_ref, s1_ref, s2_ref, rows_ref, *, tm):
    half = N_KEYS
    dn = (((1,), (1,)), ((), ()))
    s1 = lax.dot_general(keys_ref[0], q_ref[:, 0:half], dn, preferred_element_type=F32)
    s2 = lax.dot_general(keys_ref[1], q_ref[:, half:2 * half], dn, preferred_element_type=F32)
    v1 = _topk_rounds(s1, PEER_TOPK + 1)
    v2 = _topk_rounds(s2, PEER_TOPK + 1)
    v2m = jnp.concatenate(v2[:PEER_TOPK], axis=0)
    cand = jnp.concatenate([v1[a] + v2m for a in range(PEER_TOPK)], axis=0)
    top = _topk_rounds(cand, PEER_TOPK + 1)
    z = jnp.zeros((1, tm), F32)
    for tv in top[:PEER_TOPK]:
        z = z + jnp.exp(tv - top[0])
    s1_ref[...] = s1
    s2_ref[...] = s2
    next_sum = jnp.maximum(top[PEER_TOPK], jnp.maximum(v1[PEER_TOPK] + v2[0], v1[0] + v2[PEER_TOPK]))
    thr = 0.5 * (top[PEER_TOPK - 1] + next_sum)
    rows_ref[...] = jnp.concatenate([thr, v1[0], v2[0], 1.0 / z] * 2, axis=0)


ROW_THR, ROW_MAX1, ROW_MAX2, ROW_INVZ = 0, 1, 2, 3


def _peer_topk(qp, keys, tm=256):
    s = qp.shape[0]
    stat = jax.ShapeDtypeStruct((PEER_HEADS, N_KEYS, s), F32)
    stat_spec = pl.BlockSpec((None, N_KEYS, tm), lambda n, h: (h, 0, n))
    return pl.pallas_call(
        functools.partial(_peer_topk_kernel, tm=tm),
        grid=(s // tm, PEER_HEADS),
        in_specs=[pl.BlockSpec((tm, 2 * N_KEYS), lambda n, h: (n, h)),
                  pl.BlockSpec((None, 2, N_KEYS, N_KEYS), lambda n, h: (h, 0, 0, 0))],
        out_specs=[stat_spec, stat_spec, pl.BlockSpec((None, 8, tm), lambda n, h: (h, 0, n))],
        out_shape=[stat, stat, jax.ShapeDtypeStruct((PEER_HEADS, 8, s), F32)],
        compiler_params=_cparams(("parallel", "arbitrary")),
        name="peer_topk",
    )(qp, keys)


def _peer_gates(s1_ref, s2_ref, rows_ref, i1_base, n_i1, tm, live):
    w = [jnp.zeros((N_KEYS, tm), F32) for _ in range(n_i1)]
    for h in range(PEER_HEADS):
        thr = rows_ref[h, ROW_THR:ROW_THR + 1, :]
        max1 = rows_ref[h, ROW_MAX1:ROW_MAX1 + 1, :]
        max2 = rows_ref[h, ROW_MAX2:ROW_MAX2 + 1, :]
        inv_z = rows_ref[h, ROW_INVZ:ROW_INVZ + 1, :] * live
        s2 = s2_ref[h]
        e2 = jnp.exp(s2 - max2)
        for ii in range(n_i1):
            s1row = s1_ref[h, pl.ds(i1_base + ii, 1), :]
            e1row = jnp.exp(s1row - max1) * inv_z
            w[ii] = w[ii] + jnp.where(s2 >= thr - s1row, e2, 0.0) * e1row
    return w


def _peer_dense_kernel(xn_ref, down_ref, upb_ref, upa_ref, s1_ref, s2_ref, rows_ref, o_ref, hida_scr, hidb_scr,
                       *, tm, te, n_tiles):
    j = pl.program_id(1)
    half = te // 2
    n_i1 = half // N_KEYS
    dn = (((1,), (1,)), ((), ()))

    @pl.when(j == 0)
    def _():
        o_ref[...] = jnp.zeros(o_ref.shape, F32)
        hidb_scr[...] = jnp.zeros(hidb_scr.shape, F32)

    jb = jnp.maximum(j - 1, 0)
    ja = jnp.minimum(j, n_tiles - 1)
    live_a = jnp.where(j < n_tiles, 1.0, 0.0)

    def mix(w, hid):
        act = jax.nn.gelu(hid)
        return jnp.concatenate([w[ii] * act[ii * N_KEYS:(ii + 1) * N_KEYS] for ii in range(n_i1)],
                               axis=0).astype(BF16)

    wb = _peer_gates(s1_ref, s2_ref, rows_ref, (2 * jb + 1) * n_i1, n_i1, tm, 1.0)
    hida_scr[...] = lax.dot_general(down_ref[0:half, :], xn_ref[...], dn, preferred_element_type=F32)
    o_ref[...] += jnp.dot(upb_ref[...], mix(wb, hidb_scr[...]), preferred_element_type=F32)
    wa = _peer_gates(s1_ref, s2_ref, rows_ref, (2 * ja) * n_i1, n_i1, tm, live_a)
    hidb_scr[...] = lax.dot_general(down_ref[half:te, :], xn_ref[...], dn, preferred_element_type=F32)
    o_ref[...] += jnp.dot(upa_ref[...], mix(wa, hida_scr[...]), preferred_element_type=F32)


def _peer_dense(xn, down, up_t, s1, s2, rows, tm, te):
    s, d = xn.shape
    n_tiles = down.shape[0] // te
    half = te // 2
    stat_spec = pl.BlockSpec((PEER_HEADS, N_KEYS, tm), lambda n, j: (0, 0, n))
    return pl.pallas_call(
        functools.partial(_peer_dense_kernel, tm=tm, te=te, n_tiles=n_tiles),
        grid=(s // tm, n_tiles + 1),
        in_specs=[pl.BlockSpec((tm, d), lambda n, j: (n, 0)),
                  pl.BlockSpec((te, d), lambda n, j: (jnp.minimum(j, n_tiles - 1), 0)),
                  pl.BlockSpec((d, half), lambda n, j: (0, jnp.maximum(2 * j - 1, 0))),
                  pl.BlockSpec((d, half), lambda n, j: (0, jnp.minimum(2 * j, 2 * n_tiles - 1))),
                  stat_spec, stat_spec,
                  pl.BlockSpec((PEER_HEADS, 8, tm), lambda n, j: (0, 0, n))],
        out_specs=pl.BlockSpec((d, tm), lambda n, j: (0, n)),
        out_shape=jax.ShapeDtypeStruct((d, s), F32),
        scratch_shapes=[pltpu.VMEM((half, tm), F32), pltpu.VMEM((half, tm), F32)],
        compiler_params=_cparams(("parallel", "arbitrary")),
        name="peer_dense",
    )(xn, down, up_t, up_t, s1, s2, rows)


def _add_t_kernel(a_ref, bt_ref, o_ref):
    for c in range(a_ref.shape[1] // LANES):
        sl = slice(c * LANES, (c + 1) * LANES)
        o_ref[:, sl] = a_ref[:, sl] + bt_ref[sl, :].T


def _add_transposed(a, b_t, tm=256):
    s, d = a.shape
    spec = pl.BlockSpec((tm, d), lambda i: (i, 0))
    return pl.pallas_call(
        _add_t_kernel, grid=(s // tm,),
        in_specs=[spec, pl.BlockSpec((d, tm), lambda i: (0, i))], out_specs=spec,
        out_shape=jax.ShapeDtypeStruct((s, d), F32),
        compiler_params=_cparams(("parallel",)), name="residual_add",
    )(a, b_t)


def _tile(n, pref):
    t = pref
    while n % t:
        t //= 2
    return t


def _layer(x, attn_norm_g, w_in, qk_gain_a, qk_gain_b, rel_bias, cmp_pos, cmp_w1, cmp_w2,
           out_norm_g, w_out, ffn_norm_g, peer_w_query, peer_sub_keys, peer_down, peer_up):
    s, d = x.shape
    scale = HEAD_DIM ** -0.5
    ones = jnp.ones((HEAD_DIM,), F32)
    zeros = jnp.zeros((HEAD_DIM,), F32)

    xn = _rmsnorm(x, attn_norm_g)
    q_scale = scale * LOG2E
    gain_a = jnp.concatenate([qk_gain_a[0] * q_scale] * 16 + [qk_gain_a[1]] * 16 + [ones] * 16)
    mode_a = jnp.concatenate([ones] * 32 + [zeros] * 16)
    gain_b = jnp.concatenate([qk_gain_b[0] * q_scale] * 16 + [ones] * 8 + [qk_gain_b[2]] * 4 + [ones] * 4
                             + [qk_gain_b[3]] * 4 + [ones] * 4)
    mode_b = jnp.concatenate([ones] * 16 + [zeros] * 8 + [ones] * 4 + [zeros] * 4 + [ones] * 4 + [zeros] * 4)
    n_gate = w_in.shape[1] - PROJ_COLS
    w_gate = jnp.pad(w_in[:, PROJ_COLS:], ((0, 0), (0, LANES - n_gate))).astype(BF16)
    tm = _tile(s, 1024)
    dilations = tuple(dil for _, dil in DILATED_PATTERNS)
    proj_a = _proj_headnorm(xn, w_in[:, :PROJ_A_COLS].astype(BF16), gain_a.astype(F32).reshape(1, -1),
                            mode_a.reshape(1, -1), tm, 512, dilations)
    proj = _proj_headnorm(xn, w_in[:, PROJ_A_COLS:PROJ_COLS].astype(BF16), gain_b.astype(F32).reshape(1, -1),
                          mode_b.reshape(1, -1), tm, 1024)[0]
    gl = _matmul(xn, w_gate, F32, tm, LANES)

    table_a = rel_bias[:N_HEADS_A] * LOG2E
    table_b = rel_bias[N_HEADS_A:] * LOG2E

    o_dil, lse_dil = [], []
    hb = 8
    for (window, dil), arr in zip(DILATED_PATTERNS, proj_a):
        arr = arr.reshape(dil, s // dil, PROJ_A_COLS)
        bias = _band_bias(table_a, window // dil, dil, 1).reshape(N_HEADS_A // hb, hb, BAND_TQ, 2 * BAND_TQ)
        o, lse = _banded_attention(
            arr, bias, n_hb=N_HEADS_A // hb, hb=hb, r=1, nprev=1,
            qcol=lambda hh: COL_QA // hb + hh, kcol=lambda hh: COL_KA // hb + hh, vcol=lambda hh: COL_VA // hb + hh,
            with_lse=True)
        o_dil.append(o.reshape(s, N_HEADS_A * HEAD_DIM))
        lse_dil.append(lse.reshape(s, (N_HEADS_A // hb) * LANES))

    nprev_w = -(-(WIN_B - 1) // BAND_TQ)
    bias_w = _band_bias(table_b, WIN_B - 1, 1, nprev_w).reshape(N_KV_B, Q_PER_KV_B, BAND_TQ, (nprev_w + 1) * BAND_TQ)
    o_win = _banded_attention(
        proj[None], bias_w, n_hb=N_KV_B, hb=Q_PER_KV_B, r=Q_PER_KV_B, nprev=nprev_w,
        qcol=lambda hh: COL_QB // Q_PER_KV_B + hh, kcol=lambda hh: COL_KWIN + hh, vcol=lambda hh: COL_VWIN + hh,
        with_lse=False)[0]

    n_chunk = s // CMP_STRIDE
    n_cmp = (s - CMP_BLOCK) // CMP_STRIDE + 1
    raw = proj[:, COL_KCMP * HEAD_DIM:(COL_VCMP + N_KV_B) * HEAD_DIM]
    xt = raw.reshape(n_chunk, CMP_STRIDE, 2, N_KV_B, HEAD_DIM).transpose(2, 3, 1, 0, 4)
    kv_c = _compress(xt, cmp_w1.astype(BF16), cmp_w2.astype(BF16),
                     cmp_pos.reshape(2, 1, CMP_BLOCK * HEAD_DIM).astype(BF16),
                     qk_gain_b[1].reshape(1, HEAD_DIM).astype(F32))
    n_rows = CMP_PAD + n_chunk + 16
    kv_c = jnp.pad(kv_c[:, :, :n_cmp], ((0, 0), (0, 0), (CMP_PAD, n_rows - CMP_PAD - n_cmp), (0, 0)))
    kcp = kv_c[0]
    vcpt = kv_c[1].transpose(0, 2, 1)

    tl = np.arange(CMP_TQ)[None, :]
    il = np.arange(CMP_BAND)[:, None]
    rel_near = CMP_STRIDE * CMP_PAD - (CMP_BLOCK - 1) + tl - CMP_STRIDE * il
    bnear = _bias_of_rel(table_b, rel_near).reshape(N_KV_B, Q_PER_KV_B, CMP_BAND, CMP_TQ)
    b31 = table_b[:, N_BUCKETS - 1].astype(F32)
    n_slc = s // SLC_BLOCK
    ratio = SLC_BLOCK // CMP_STRIDE
    tt_np = np.zeros((n_slc, n_rows), np.float32)
    for jblk in range(n_slc):
        for off, wgt in ((-1, 1.0), (0, 2.0), (1, 2.0), (2, 2.0), (3, 1.0)):
            i_c = ratio * jblk + off
            if 0 <= i_c < n_cmp:
                tt_np[jblk, CMP_PAD + i_c] = wgt
    tt = jnp.asarray(tt_np, BF16)
    o_cmp, mask = _cmp_select(proj, kcp, vcpt, bnear, b31, tt)

    vt = proj[:, COL_VSLC * HEAD_DIM:(COL_VSLC + N_KV_B) * HEAD_DIM].reshape(s, N_KV_B, HEAD_DIM).transpose(1, 2, 0)
    vt = jnp.concatenate([vt, jnp.ones((N_KV_B, 16, s), BF16)], axis=1)
    kl = np.arange(CMP_TQ)[:, None]
    od = np.arange(SLC_NEAR)[:, None, None]
    rel_s = CMP_TQ * od + tl[None] - kl[None]
    btile = _bias_of_rel(table_b, rel_s).reshape(N_KV_B, Q_PER_KV_B, SLC_NEAR, CMP_TQ, CMP_TQ)
    btile = btile.transpose(0, 2, 3, 1, 4).reshape(N_KV_B, SLC_NEAR, CMP_TQ, Q_PER_KV_B * CMP_TQ)
    o_slc = _slc_attention(proj, vt, mask, btile)

    o_n = _combine(o_dil, lse_dil, o_cmp, o_slc, o_win, gl, out_norm_g.astype(F32))
    h = _matmul_residual(o_n, w_out.astype(BF16), x, tm, _tile(d, 1024))

    xn2 = _rmsnorm(h, ffn_norm_g)
    qp = _matmul(xn2, peer_w_query.astype(BF16), BF16, tm, 1024)
    s1, s2, rows = _peer_topk(qp, peer_sub_keys.astype(BF16))
    peer_t = _peer_dense(xn2, peer_down.astype(BF16), peer_up.T.astype(BF16), s1, s2, rows, _tile(s, 512), 512)
    return _add_transposed(h, peer_t)


def kernel(x, attn_norm_g, w_in, qk_gain_a, qk_gain_b, rel_bias, cmp_pos, cmp_w1, cmp_w2, out_norm_g, w_out,
           ffn_norm_g, peer_w_query, peer_sub_keys, peer_down, peer_up):
    b = x.shape[0]
    outs = [_layer(x[bi], attn_norm_g[0], w_in[0], qk_gain_a[0], qk_gain_b[0], rel_bias, cmp_pos[0], cmp_w1[0],
                   cmp_w2[0], out_norm_g[0], w_out[0], ffn_norm_g[0], peer_w_query[0], peer_sub_keys[0],
                   peer_down[0], peer_up[0]) for bi in range(b)]
    return jnp.stack(outs, axis=0)
```

```python
import functools
import math

import jax
import jax.numpy as jnp
import numpy as np
from jax import lax
from jax.experimental import pallas as pl
from jax.experimental.pallas import tpu as pltpu

F32 = jnp.float32
BF16 = jnp.bfloat16

HEAD_DIM = 128
LANES = 128
N_HEADS_A = 16
N_HEADS_B = 16
N_KV_B = 4
Q_PER_KV_B = 4
DILATED_PATTERNS = ((128, 1), (512, 4), (2048, 16))
CMP_BLOCK = 32
CMP_STRIDE = 16
SLC_BLOCK = 64
N_SLC = 16
WIN_B = 512
N_BUCKETS = 32
MAX_DISTANCE = 2048
PEER_HEADS = 8
N_KEYS = 128
PEER_TOPK = 16
RMS_EPS = 1e-6
NEG = -1e30
BIG = 3e38
LOG2E = math.log2(math.e)

COL_QA, COL_KA, COL_VA = 0, 16, 32
N_PROJ_A_HEADS = 48
COL_QB, COL_KCMP, COL_VCMP, COL_KSLC, COL_VSLC, COL_KWIN, COL_VWIN = 0, 16, 20, 24, 28, 32, 36
N_PROJ_B_HEADS = 40
PROJ_A_COLS = N_PROJ_A_HEADS * HEAD_DIM
PROJ_COLS = (N_PROJ_A_HEADS + N_PROJ_B_HEADS) * HEAD_DIM

BAND_TQ = 128
CMP_TQ = 256
CMP_PAD = 112
CMP_BAND = 128
SLC_NEAR = 8
SLC_AUG = 16
CMP_AUG = 128
VMEM_LIMIT = 56 * 1024 * 1024


def _cparams(sem):
    return pltpu.CompilerParams(dimension_semantics=sem, vmem_limit_bytes=VMEM_LIMIT)


def _rel_bucket(dist):
    n = np.maximum(dist, 0)
    max_exact = N_BUCKETS // 2
    nf = np.maximum(n, 1).astype(np.float32)
    log_part = (np.log(nf / np.float32(max_exact)) / np.float32(math.log(MAX_DISTANCE / max_exact))
                * np.float32(N_BUCKETS - max_exact))
    large = np.minimum(max_exact + log_part.astype(np.int32), N_BUCKETS - 1)
    return np.where(n < max_exact, n, large).astype(np.int32)


def _bias_of_rel(table, rel):
    rel = np.asarray(rel)
    bucket = np.where(rel >= 0, _rel_bucket(rel), N_BUCKETS).reshape(-1)
    onehot = (jnp.asarray(bucket)[None, :] == jnp.arange(N_BUCKETS + 1)[:, None]).astype(F32)
    table = jnp.concatenate([table.astype(F32), jnp.full((table.shape[0], 1), NEG, F32)], axis=1)
    out = jnp.dot(table, onehot, precision=lax.Precision.HIGHEST)
    return out.reshape((table.shape[0],) + rel.shape)


def _rmsnorm_kernel(x_ref, g_ref, o_ref):
    x = x_ref[...]
    ms = jnp.mean(x * x, axis=-1, keepdims=True)
    o_ref[...] = (x * lax.rsqrt(ms + RMS_EPS) * g_ref[...]).astype(o_ref.dtype)


def _rmsnorm(x, g, tm=256):
    s, d = x.shape
    return pl.pallas_call(
        _rmsnorm_kernel,
        grid=(s // tm,),
        in_specs=[pl.BlockSpec((tm, d), lambda i: (i, 0)), pl.BlockSpec((1, d), lambda i: (0, 0))],
        out_specs=pl.BlockSpec((tm, d), lambda i: (i, 0)),
        out_shape=jax.ShapeDtypeStruct((s, d), BF16),
        compiler_params=_cparams(("parallel",)),
        name="rmsnorm",
    )(x, g.reshape(1, d).astype(F32))


def _proj_kernel(x_ref, w_ref, gain_ref, mode_ref, *refs, n_chunks, dilations, tm):
    outs = refs[:len(dilations)]
    acc = jnp.dot(x_ref[...], w_ref[...], preferred_element_type=F32)
    for c in range(n_chunks):
        sl = slice(c * LANES, (c + 1) * LANES)
        t = acc[:, sl]
        ms = jnp.mean(t * t, axis=-1, keepdims=True)
        mult = jnp.where(mode_ref[:, sl] > 0, lax.rsqrt(ms + RMS_EPS) * gain_ref[:, sl], 1.0)
        y = t * mult
        if len(dilations) > 1:
            y_scr = refs[len(dilations)]
            y_scr[c] = y
        for o_ref, dil in zip(outs, dilations):
            if dil == 1:
                o_ref[:, sl] = y.astype(o_ref.dtype)
            else:
                for r in range(dil):
                    o_ref[r, :, sl] = y_scr[c, pl.ds(r, tm // dil, stride=dil), :].astype(o_ref.dtype)


def _proj_headnorm(xn, w, gain, mode, tm, tn, dilations=(1,)):
    s, d = xn.shape
    n = w.shape[1]
    out_specs, out_shape = [], []
    for dil in dilations:
        if dil == 1:
            out_specs.append(pl.BlockSpec((tm, tn), lambda i, j: (i, j)))
            out_shape.append(jax.ShapeDtypeStruct((s, n), BF16))
        else:
            out_specs.append(pl.BlockSpec((dil, tm // dil, tn), lambda i, j: (0, i, j)))
            out_shape.append(jax.ShapeDtypeStruct((dil, s // dil, n), BF16))
    scratch = [pltpu.VMEM((tn // LANES, tm, LANES), F32)] if len(dilations) > 1 else []
    return pl.pallas_call(
        functools.partial(_proj_kernel, n_chunks=tn // LANES, dilations=tuple(dilations), tm=tm),
        grid=(s // tm, n // tn),
        in_specs=[pl.BlockSpec((tm, d), lambda i, j: (i, 0)),
                  pl.BlockSpec((d, tn), lambda i, j: (0, j)),
                  pl.BlockSpec((1, tn), lambda i, j: (0, j)),
                  pl.BlockSpec((1, tn), lambda i, j: (0, j))],
        out_specs=out_specs,
        out_shape=out_shape,
        scratch_shapes=scratch,
        compiler_params=_cparams(("parallel", "arbitrary")),
        name="proj_headnorm",
    )(xn, w, gain, mode)


def _matmul_kernel(x_ref, w_ref, o_ref):
    o_ref[...] = jnp.dot(x_ref[...], w_ref[...], preferred_element_type=F32).astype(o_ref.dtype)


def _matmul(x, w, out_dtype, tm, tn):
    s, d = x.shape
    n = w.shape[1]
    return pl.pallas_call(
        _matmul_kernel,
        grid=(s // tm, n // tn),
        in_specs=[pl.BlockSpec((tm, d), lambda i, j: (i, 0)), pl.BlockSpec((d, tn), lambda i, j: (0, j))],
        out_specs=pl.BlockSpec((tm, tn), lambda i, j: (i, j)),
        out_shape=jax.ShapeDtypeStruct((s, n), out_dtype),
        compiler_params=_cparams(("parallel", "arbitrary")),
        name="matmul",
    )(x, w)


def _matmul_res_kernel(x_ref, w_ref, r_ref, o_ref):
    o_ref[...] = r_ref[...] + jnp.dot(x_ref[...], w_ref[...], preferred_element_type=F32)


def _matmul_residual(x, w, res, tm, tn):
    s, d = x.shape
    n = w.shape[1]
    return pl.pallas_call(
        _matmul_res_kernel,
        grid=(s // tm, n // tn),
        in_specs=[pl.BlockSpec((tm, d), lambda i, j: (i, 0)),
                  pl.BlockSpec((d, tn), lambda i, j: (0, j)),
                  pl.BlockSpec((tm, tn), lambda i, j: (i, j))],
        out_specs=pl.BlockSpec((tm, tn), lambda i, j: (i, j)),
        out_shape=jax.ShapeDtypeStruct((s, n), F32),
        compiler_params=_cparams(("parallel", "arbitrary")),
        name="matmul_residual",
    )(x, w, res)


def _banded_kernel(*refs, hb, r, nprev, tq, with_lse):
    nk = nprev + 1
    q_ref = refs[0]
    k_refs = refs[1:1 + nk]
    v_refs = refs[1 + nk:1 + 2 * nk]
    bias_ref = refs[1 + 2 * nk]
    o_ref = refs[2 + 2 * nk]
    i = pl.program_id(2)
    span = nk * tq
    ng = hb // r
    q3 = jnp.stack([jnp.concatenate([q_ref[:, (g * r + rr) * LANES:(g * r + rr + 1) * LANES] for rr in range(r)],
                                    axis=0) for g in range(ng)], axis=0)
    parts = []
    for j in range(nk):
        k3 = jnp.stack([k_refs[j][:, g * LANES:(g + 1) * LANES] for g in range(ng)], axis=0)
        sj = jnp.einsum('gqd,gkd->gqk', q3, k3, preferred_element_type=F32)
        if j < nprev:
            sj = sj + jnp.where(i >= nprev - j, 0.0, NEG)
        parts.append(sj)
    s = jnp.concatenate(parts, axis=2) + bias_ref[...].reshape(ng, r * tq, span)
    m = jnp.max(s, axis=-1, keepdims=True)
    p = jnp.exp2(s - m)
    den = jnp.sum(p, axis=-1, keepdims=True)
    pb = p.astype(BF16)
    o = None
    for j in range(nk):
        v3 = jnp.stack([v_refs[j][:, g * LANES:(g + 1) * LANES] for g in range(ng)], axis=0)
        oj = jnp.einsum('gqk,gkd->gqd', pb[:, :, j * tq:(j + 1) * tq], v3, preferred_element_type=F32)
        o = oj if o is None else o + oj
    o = o / den
    for g in range(ng):
        for rr in range(r):
            h = g * r + rr
            o_ref[:, h * LANES:(h + 1) * LANES] = o[g, rr * tq:(rr + 1) * tq].astype(o_ref.dtype)
    if with_lse:
        lse_ref = refs[3 + 2 * nk]
        lse = m + jnp.log(den) * LOG2E
        lane = lax.broadcasted_iota(jnp.int32, (tq, LANES), 1)
        lse_mat = jnp.zeros((tq, LANES), F32)
        for g in range(ng):
            lse_mat = jnp.where(lane == g, lse[g], lse_mat)
        lse_ref[...] = lse_mat


def _banded_attention(arr, bias, *, n_hb, hb, r, nprev, qcol, kcol, vcol, with_lse):
    tq = BAND_TQ
    n_r, length = arr.shape[0], arr.shape[1]
    nk = nprev + 1
    qw = hb * LANES
    kw = (hb // r) * LANES

    def kmap(col, back):
        return lambda rr, hh, i: (rr, jnp.maximum(i - back, 0), col(hh))

    in_specs = [pl.BlockSpec((None, tq, qw), lambda rr, hh, i: (rr, i, qcol(hh)))]
    in_specs += [pl.BlockSpec((None, tq, kw), kmap(kcol, nprev - j)) for j in range(nk)]
    in_specs += [pl.BlockSpec((None, tq, kw), kmap(vcol, nprev - j)) for j in range(nk)]
    in_specs += [pl.BlockSpec((None, hb, tq, nk * tq), lambda rr, hh, i: (hh, 0, 0, 0))]
    out_specs = [pl.BlockSpec((tq, qw), lambda rr, hh, i: (i, rr * n_hb + hh))]
    out_shape = [jax.ShapeDtypeStruct((length, n_r * n_hb * qw), BF16)]
    if with_lse:
        out_specs.append(pl.BlockSpec((tq, LANES), lambda rr, hh, i: (i, rr * n_hb + hh)))
        out_shape.append(jax.ShapeDtypeStruct((length, n_r * n_hb * LANES), F32))
    return pl.pallas_call(
        functools.partial(_banded_kernel, hb=hb, r=r, nprev=nprev, tq=tq, with_lse=with_lse),
        grid=(n_r, n_hb, length // tq),
        in_specs=in_specs,
        out_specs=out_specs,
        out_shape=out_shape,
        compiler_params=_cparams(("parallel", "parallel", "arbitrary")),
        name="banded_attention",
    )(*([arr] * (1 + 2 * nk)), bias)


def _band_bias(table, max_dist, dist_scale, nprev):
    tq = BAND_TQ
    q_loc = np.arange(tq)[:, None]
    k_loc = np.arange((nprev + 1) * tq)[None, :] - nprev * tq
    rel = q_loc - k_loc
    return _bias_of_rel(table, np.where((rel >= 0) & (rel <= max_dist), rel * dist_scale, -1))


def _compress_kernel(x_ref, w1_ref, w2_ref, pos_ref, gain_ref, o_ref, *, n_chunk):
    half = CMP_STRIDE
    a = jnp.zeros((n_chunk, HEAD_DIM), F32)
    b = jnp.zeros((n_chunk, HEAD_DIM), F32)
    for c in range(half):
        xc = x_ref[c]
        a = a + jnp.dot(xc, w1_ref[c * HEAD_DIM:(c + 1) * HEAD_DIM, :], preferred_element_type=F32)
        b = b + jnp.dot(xc, w1_ref[(half + c) * HEAD_DIM:(half + c + 1) * HEAD_DIM, :],
                        preferred_element_type=F32)
    pos = jnp.broadcast_to(pos_ref[...], (8, CMP_BLOCK * HEAD_DIM))
    posterm = jnp.dot(pos, w1_ref[...], preferred_element_type=F32)[0:1, :]
    pre = a + pltpu.roll(b, n_chunk - 1, 0) + posterm
    hid = jax.nn.gelu(pre)
    out = jnp.dot(hid.astype(BF16), w2_ref[...], preferred_element_type=F32)
    ms = jnp.mean(out * out, axis=-1, keepdims=True)
    normed = out * lax.rsqrt(ms + RMS_EPS) * gain_ref[...]
    o_ref[...] = jnp.where(pl.program_id(0) == 0, normed, out).astype(o_ref.dtype)


def _compress(xt, w1, w2, pos, gain):
    n_chunk = xt.shape[3]
    return pl.pallas_call(
        functools.partial(_compress_kernel, n_chunk=n_chunk),
        grid=(2, N_KV_B),
        in_specs=[pl.BlockSpec((None, None, CMP_STRIDE, n_chunk, HEAD_DIM), lambda w, g: (w, g, 0, 0, 0)),
                  pl.BlockSpec((None, CMP_BLOCK * HEAD_DIM, HEAD_DIM), lambda w, g: (w, 0, 0)),
                  pl.BlockSpec((None, HEAD_DIM, HEAD_DIM), lambda w, g: (w, 0, 0)),
                  pl.BlockSpec((None, 1, CMP_BLOCK * HEAD_DIM), lambda w, g: (w, 0, 0)),
                  pl.BlockSpec((1, HEAD_DIM), lambda w, g: (0, 0))],
        out_specs=pl.BlockSpec((None, None, n_chunk, HEAD_DIM), lambda w, g: (w, g, 0, 0)),
        out_shape=jax.ShapeDtypeStruct((2, N_KV_B, n_chunk, HEAD_DIM), BF16),
        compiler_params=_cparams(("parallel", "parallel")),
        name="nsa_compress",
    )(xt, w1, w2, pos, gain)


def _cmp_select_kernel(q_ref, kc_ref, vct_ref, bnear_ref, bfar_ref, tt_ref, o_ref, mask_ref, p_scr, *, tq):
    n = pl.program_id(1)
    t0 = n * tq
    wide = Q_PER_KV_B * tq
    band_lo = pl.multiple_of(n * (tq // CMP_STRIDE), 16)
    t = t0 + lax.broadcasted_iota(jnp.int32, (1, tq), 1)
    valid = jnp.concatenate([t >= CMP_BLOCK - 1] * Q_PER_KV_B, axis=1)
    q_t = jnp.concatenate([q_ref[:, r * LANES:(r + 1) * LANES].astype(F32).T.astype(BF16)
                           for r in range(Q_PER_KV_B)], axis=1)
    ri = lax.broadcasted_iota(jnp.int32, (CMP_AUG, wide), 0)
    blocked = (ri == 2) | ((ri >= 3) & (ri - 3 >= n * (tq // CMP_STRIDE) // 16))
    aug = jnp.where(ri == 0, bfar_ref[0:1, :], jnp.where(ri == 1, bfar_ref[1:2, :],
                                                         jnp.where(blocked, NEG, 0.0)))
    rhs = jnp.concatenate([q_t, aug.astype(BF16)], axis=0)
    s_far = jnp.dot(kc_ref[...], rhs, preferred_element_type=F32)
    brow = lax.broadcasted_iota(jnp.int32, (CMP_BAND, wide), 0) + band_lo
    s_band = jnp.dot(kc_ref[pl.ds(band_lo, CMP_BAND), 0:HEAD_DIM], q_t, preferred_element_type=F32)
    s_band = s_band + bnear_ref[...] + jnp.where(brow >= CMP_PAD, 0.0, NEG)
    m = jnp.maximum(jnp.max(s_far, axis=0, keepdims=True), jnp.max(s_band, axis=0, keepdims=True))
    p_far = jnp.exp2(s_far - m)
    p_band = jnp.exp2(s_band - m)
    den = jnp.sum(p_far, axis=0, keepdims=True) + jnp.sum(p_band, axis=0, keepdims=True)
    inv = jnp.where(valid, 1.0 / den, 0.0)
    p_scr[...] = p_far * inv
    p_scr[pl.ds(band_lo, CMP_BAND), :] = p_band * inv
    p = p_scr[...]
    o_t = jnp.dot(vct_ref[...], p.astype(BF16), preferred_element_type=F32)
    psum = p[:, 0:tq]
    for r in range(Q_PER_KV_B):
        o_ref[:, r * LANES:(r + 1) * LANES] = o_t[:, r * tq:(r + 1) * tq].T.astype(o_ref.dtype)
        if r:
            psum = psum + p[:, r * tq:(r + 1) * tq]
    p_hi = psum.astype(BF16)
    p_lo = (psum - p_hi.astype(F32)).astype(BF16)
    imp = (jnp.dot(tt_ref[...], p_hi, preferred_element_type=F32)
           + jnp.dot(tt_ref[...], p_lo, preferred_element_type=F32))
    n_slc = imp.shape[0]
    jj = lax.broadcasted_iota(jnp.int32, (n_slc, tq), 0)
    cur = t // SLC_BLOCK
    allowed = jj <= cur
    score = jnp.where(jj == 0, 3e9,
                      jnp.where(jj == cur, 2e9,
                                jnp.where(jj == cur - 1, 1e9, jnp.where(allowed, imp, NEG))))
    work = score
    thr = jnp.zeros((1, tq), F32)
    for _ in range(N_SLC):
        thr = jnp.max(work, axis=0, keepdims=True)
        work = jnp.where(work >= thr, -BIG, work)
    mask_ref[...] = jnp.where((score >= thr) & allowed, 0.0, NEG)


def _cmp_select(proj, kc_aug, vcpt, bnear, bfar, tt):
    s = proj.shape[0]
    tq = CMP_TQ
    wide = Q_PER_KV_B * tq
    n_rows = kc_aug.shape[1]
    n_slc = tt.shape[0]
    return pl.pallas_call(
        functools.partial(_cmp_select_kernel, tq=tq),
        grid=(N_KV_B, s // tq),
        in_specs=[pl.BlockSpec((tq, Q_PER_KV_B * LANES), lambda g, n: (n, COL_QB // Q_PER_KV_B + g)),
                  pl.BlockSpec((None, n_rows, HEAD_DIM + CMP_AUG), lambda g, n: (g, 0, 0)),
                  pl.BlockSpec((None, HEAD_DIM, n_rows), lambda g, n: (g, 0, 0)),
                  pl.BlockSpec((None, CMP_BAND, wide), lambda g, n: (g, 0, 0)),
                  pl.BlockSpec((None, 8, wide), lambda g, n: (g, 0, 0)),
                  pl.BlockSpec((n_slc, n_rows), lambda g, n: (0, 0))],
        out_specs=[pl.BlockSpec((tq, Q_PER_KV_B * LANES), lambda g, n: (n, g)),
                   pl.BlockSpec((None, n_slc, tq), lambda g, n: (g, 0, n))],
        out_shape=[jax.ShapeDtypeStruct((s, N_HEADS_B * HEAD_DIM), BF16),
                   jax.ShapeDtypeStruct((N_KV_B, n_slc, s), F32)],
        scratch_shapes=[pltpu.VMEM((n_rows, wide), F32)],
        compiler_params=_cparams(("parallel", "arbitrary")),
        name="nsa_cmp_select",
    )(proj, kc_aug, vcpt, bnear, bfar, tt)


def _slc_kernel(q_ref, k_ref, vt_ref, mask_ref, bdiff_ref, bfar_ref, o_ref,
                s_scr, p_scr, alpha_scr, m_scr, acc_scr, *, tq):
    n = pl.program_id(1)
    last = n
    blocks_per_chunk = tq // SLC_BLOCK
    wide = Q_PER_KV_B * tq
    q_t = jnp.concatenate([q_ref[:, r * LANES:(r + 1) * LANES].astype(F32).T.astype(BF16)
                           for r in range(Q_PER_KV_B)], axis=1)
    tail_rows = [bfar_ref[0:1, :], bfar_ref[1:2, :],
                 jnp.zeros((SLC_AUG - blocks_per_chunk - 2, wide), F32)]

    def scores(c, slot, near):
        pen = jnp.where(c > last, NEG, 0.0)
        c = jnp.minimum(c, last)
        k0 = pl.multiple_of(c * tq, tq)
        mrows = [jnp.concatenate([mask_ref[pl.ds(c * blocks_per_chunk + b, 1), :] + pen] * Q_PER_KV_B, axis=1)
                 for b in range(blocks_per_chunk)]
        aug = jnp.concatenate(mrows + tail_rows, axis=0).astype(BF16)
        rhs = jnp.concatenate([q_t, aug], axis=0)
        s = jnp.dot(k_ref[pl.ds(k0, tq), :], rhs, preferred_element_type=F32)
        if near:
            s = s + bdiff_ref[jnp.minimum(n - c, SLC_NEAR - 1)]
        s_scr[slot] = s

    def softmax_update(slot):
        s = s_scr[slot]
        m_old = m_scr[...]
        m_new = jnp.maximum(m_old, jnp.max(s, axis=0, keepdims=True))
        alpha = jnp.exp2(m_old - m_new)
        p = jnp.exp2(s - m_new)
        m_scr[...] = m_new
        alpha_scr[slot] = alpha
        p_scr[slot] = p.astype(BF16)

    def values(c, slot):
        k0 = pl.multiple_of(jnp.clip(c, 0, last) * tq, tq)
        acc_scr[...] = alpha_scr[slot] * acc_scr[...] + jnp.dot(
            vt_ref[:, pl.ds(k0, tq)], p_scr[slot], preferred_element_type=F32)

    m_scr[...] = jnp.full(m_scr.shape, -BIG, F32)
    acc_scr[...] = jnp.zeros(acc_scr.shape, F32)
    p_scr[1] = jnp.zeros(p_scr.shape[1:], BF16)
    alpha_scr[1] = jnp.ones(alpha_scr.shape[1:], F32)
    scores(0, 0, True)

    def body(i, near):
        c = 2 * i
        scores(c + 1, 1, near)
        softmax_update(0)
        values(c - 1, 1)
        scores(c + 2, 0, near)
        softmax_update(1)
        values(c, 0)

    def far_body(i, carry):
        body(i, False)
        return carry

    def near_body(i, carry):
        body(i, True)
        return carry

    n_pairs = (last + 2) // 2
    n_far = jnp.clip((n - (SLC_NEAR - 1)) // 2, 0, n_pairs)
    lax.fori_loop(0, n_far, far_body, 0)
    lax.fori_loop(n_far, n_pairs, near_body, 0)
    values(2 * n_pairs - 1, 1)
    o_t = acc_scr[0:HEAD_DIM, :] / acc_scr[HEAD_DIM:HEAD_DIM + 1, :]
    for r in range(Q_PER_KV_B):
        o_ref[:, r * LANES:(r + 1) * LANES] = o_t[:, r * tq:(r + 1) * tq].T.astype(o_ref.dtype)


def _slc_attention(proj, k_aug, vt, mask, bdiff, bfar):
    s = proj.shape[0]
    tq = CMP_TQ
    n_slc = mask.shape[1]
    wide = Q_PER_KV_B * tq
    v_rows = vt.shape[1]
    return pl.pallas_call(
        functools.partial(_slc_kernel, tq=tq),
        grid=(N_KV_B, s // tq),
        in_specs=[pl.BlockSpec((tq, Q_PER_KV_B * LANES), lambda g, n: (n, COL_QB // Q_PER_KV_B + g)),
                  pl.BlockSpec((None, s, HEAD_DIM + SLC_AUG), lambda g, n: (g, 0, 0)),
                  pl.BlockSpec((None, v_rows, s), lambda g, n: (g, 0, 0)),
                  pl.BlockSpec((None, n_slc, tq), lambda g, n: (g, 0, n)),
                  pl.BlockSpec((None, SLC_NEAR, tq, wide), lambda g, n: (g, 0, 0, 0)),
                  pl.BlockSpec((None, 8, wide), lambda g, n: (g, 0, 0))],
        out_specs=pl.BlockSpec((tq, Q_PER_KV_B * LANES), lambda g, n: (n, g)),
        out_shape=jax.ShapeDtypeStruct((s, N_HEADS_B * HEAD_DIM), BF16),
        scratch_shapes=[pltpu.VMEM((2, tq, wide), F32),
                        pltpu.VMEM((2, tq, wide), BF16),
                        pltpu.VMEM((2, 1, wide), F32),
                        pltpu.VMEM((1, wide), F32),
                        pltpu.VMEM((v_rows, wide), F32)],
        compiler_params=_cparams(("parallel", "arbitrary")),
        name="nsa_selected",
    )(proj, k_aug, vt, mask, bdiff, bfar)


def _combine_kernel(o1_ref, o2_ref, o3_ref, l1_ref, l2_ref, l3_ref, oc_ref, os_ref, ow_ref, gl_ref, g_ref, o_ref):
    heads_per_lse_block = 8
    for h in range(N_HEADS_A):
        sl = slice(h * LANES, (h + 1) * LANES)
        col = (h // heads_per_lse_block) * LANES + h % heads_per_lse_block
        l1 = l1_ref[:, col:col + 1]
        l2 = l2_ref[:, col:col + 1]
        l3 = l3_ref[:, col:col + 1]
        m = jnp.maximum(jnp.maximum(l1, l2), l3)
        e1, e2, e3 = jnp.exp2(l1 - m), jnp.exp2(l2 - m), jnp.exp2(l3 - m)
        den = e1 + e2 + e3
        o = (o1_ref[:, sl].astype(F32) * (e1 / den) + o2_ref[:, sl].astype(F32) * (e2 / den)
             + o3_ref[:, sl].astype(F32) * (e3 / den))
        ms = jnp.mean(o * o, axis=-1, keepdims=True)
        o_ref[:, sl] = (o * lax.rsqrt(ms + RMS_EPS) * g_ref[h:h + 1, :]).astype(o_ref.dtype)
    gates = jax.nn.sigmoid(gl_ref[...])
    for h in range(N_HEADS_B):
        sl = slice(h * LANES, (h + 1) * LANES)
        o = (gates[:, 3 * h:3 * h + 1] * oc_ref[:, sl].astype(F32)
             + gates[:, 3 * h + 1:3 * h + 2] * os_ref[:, sl].astype(F32)
             + gates[:, 3 * h + 2:3 * h + 3] * ow_ref[:, sl].astype(F32))
        ms = jnp.mean(o * o, axis=-1, keepdims=True)
        hh = N_HEADS_A + h
        o_ref[:, hh * LANES:(hh + 1) * LANES] = (
            o * lax.rsqrt(ms + RMS_EPS) * g_ref[hh:hh + 1, :]).astype(o_ref.dtype)


def _combine(o_dil, lse_dil, o_cmp, o_slc, o_win, gl, out_gain, tm=256):
    s = o_cmp.shape[0]
    wa = N_HEADS_A * HEAD_DIM
    wide = pl.BlockSpec((tm, wa), lambda i: (i, 0))
    lse_spec = pl.BlockSpec((tm, 2 * LANES), lambda i: (i, 0))
    return pl.pallas_call(
        _combine_kernel,
        grid=(s // tm,),
        in_specs=[wide, wide, wide, lse_spec, lse_spec, lse_spec, wide, wide, wide,
                  pl.BlockSpec((tm, LANES), lambda i: (i, 0)),
                  pl.BlockSpec((N_HEADS_A + N_HEADS_B, HEAD_DIM), lambda i: (0, 0))],
        out_specs=pl.BlockSpec((tm, 2 * wa), lambda i: (i, 0)),
        out_shape=jax.ShapeDtypeStruct((s, 2 * wa), BF16),
        compiler_params=_cparams(("parallel",)),
        name="combine_headnorm",
    )(*o_dil, *lse_dil, o_cmp, o_slc, o_win, gl, out_gain)


def _topk_rounds(work, k):
    vals = []
    for _ in range(k):
        mx = jnp.max(work, axis=0, keepdims=True)
        vals.append(mx)
        work = jnp.where(work >= mx, -BIG, work)
    return vals


def _peer_topk_kernel(q_ref, keys_ref, s1_ref, s2_ref, rows_ref, *, tm):
    half = N_KEYS
    dn = (((1,), (1,)), ((), ()))
    s1 = lax.dot_general(keys_ref[0], q_ref[:, 0:half], dn, preferred_element_type=F32)
    s2 = lax.dot_general(keys_ref[1], q_ref[:, half:2 * half], dn, preferred_element_type=F32)
    v1 = _topk_rounds(s1, PEER_TOPK + 1)
    v2 = _topk_rounds(s2, PEER_TOPK + 1)
    v2m = jnp.concatenate(v2[:PEER_TOPK], axis=0)
    cand = jnp.concatenate([v1[a] + v2m for a in range(PEER_TOPK)], axis=0)
    top = _topk_rounds(cand, PEER_TOPK + 1)
    z = jnp.zeros((1, tm), F32)
    for tv in top[:PEER_TOPK]:
        z = z + jnp.exp(tv - top[0])
    s1_ref[...] = s1
    s2_ref[...] = s2
    next_sum = jnp.maximum(top[PEER_TOPK], jnp.maximum(v1[PEER_TOPK] + v2[0], v1[0] + v2[PEER_TOPK]))
    thr = 0.5 * (top[PEER_TOPK - 1] + next_sum)
    rows_ref[...] = jnp.concatenate([thr, v1[0], v2[0], 1.0 / z] * 2, axis=0)


ROW_THR, ROW_MAX1, ROW_MAX2, ROW_INVZ = 0, 1, 2, 3


def _peer_topk(qp, keys, tm=256):
    s = qp.shape[0]
    stat = jax.ShapeDtypeStruct((PEER_HEADS, N_KEYS, s), F32)
    stat_spec = pl.BlockSpec((None, N_KEYS, tm), lambda n, h: (h, 0, n))
    return pl.pallas_call(
        functools.partial(_peer_topk_kernel, tm=tm),
        grid=(s // tm, PEER_HEADS),
        in_specs=[pl.BlockSpec((tm, 2 * N_KEYS), lambda n, h: (n, h)),
                  pl.BlockSpec((None, 2, N_KEYS, N_KEYS), lambda n, h: (h, 0, 0, 0))],
        out_specs=[stat_spec, stat_spec, pl.BlockSpec((None, 8, tm), lambda n, h: (h, 0, n))],
        out_shape=[stat, stat, jax.ShapeDtypeStruct((PEER_HEADS, 8, s), F32)],
        compiler_params=_cparams(("parallel", "arbitrary")),
        name="peer_topk",
    )(qp, keys)


def _peer_gates(s1_ref, s2_ref, rows_ref, i1_base, n_i1, tm, live):
    w = [jnp.zeros((N_KEYS, tm), F32) for _ in range(n_i1)]
    for h in range(PEER_HEADS):
        thr = rows_ref[h, ROW_THR:ROW_THR + 1, :]
        max1 = rows_ref[h, ROW_MAX1:ROW_MAX1 + 1, :]
        max2 = rows_ref[h, ROW_MAX2:ROW_MAX2 + 1, :]
        inv_z = rows_ref[h, ROW_INVZ:ROW_INVZ + 1, :] * live
        s2 = s2_ref[h]
        e2 = jnp.exp(s2 - max2)
        for ii in range(n_i1):
            s1row = s1_ref[h, pl.ds(i1_base + ii, 1), :]
            e1row = jnp.exp(s1row - max1) * inv_z
            w[ii] = w[ii] + jnp.where(s2 >= thr - s1row, e2, 0.0) * e1row
    return w


def _peer_dense_kernel(xn_ref, down_ref, upb_ref, upa_ref, s1_ref, s2_ref, rows_ref, o_ref, hida_scr, hidb_scr,
                       *, tm, te, n_tiles):
    j = pl.program_id(1)
    half = te // 2
    n_i1 = half // N_KEYS
    dn = (((1,), (1,)), ((), ()))

    @pl.when(j == 0)
    def _():
        o_ref[...] = jnp.zeros(o_ref.shape, F32)
        hidb_scr[...] = jnp.zeros(hidb_scr.shape, F32)

    jb = jnp.maximum(j - 1, 0)
    ja = jnp.minimum(j, n_tiles - 1)
    live_a = jnp.where(j < n_tiles, 1.0, 0.0)

    def mix(w, hid):
        act = jax.nn.gelu(hid)
        return jnp.concatenate([w[ii] * act[ii * N_KEYS:(ii + 1) * N_KEYS] for ii in range(n_i1)],
                               axis=0).astype(BF16)

    wb = _peer_gates(s1_ref, s2_ref, rows_ref, (2 * jb + 1) * n_i1, n_i1, tm, 1.0)
    hida_scr[...] = lax.dot_general(down_ref[0:half, :], xn_ref[...], dn, preferred_element_type=F32)
    o_ref[...] += jnp.dot(upb_ref[...], mix(wb, hidb_scr[...]), preferred_element_type=F32)
    wa = _peer_gates(s1_ref, s2_ref, rows_ref, (2 * ja) * n_i1, n_i1, tm, live_a)
    hidb_scr[...] = lax.dot_general(down_ref[half:te, :], xn_ref[...], dn, preferred_element_type=F32)
    o_ref[...] += jnp.dot(upa_ref[...], mix(wa, hida_scr[...]), preferred_element_type=F32)


def _peer_dense(xn, down, up_t, s1, s2, rows, tm, te):
    s, d = xn.shape
    n_tiles = down.shape[0] // te
    half = te // 2
    stat_spec = pl.BlockSpec((PEER_HEADS, N_KEYS, tm), lambda n, j: (0, 0, n))
    return pl.pallas_call(
        functools.partial(_peer_dense_kernel, tm=tm, te=te, n_tiles=n_tiles),
        grid=(s // tm, n_tiles + 1),
        in_specs=[pl.BlockSpec((tm, d), lambda n, j: (n, 0)),
                  pl.BlockSpec((te, d), lambda n, j: (jnp.minimum(j, n_tiles - 1), 0)),
                  pl.BlockSpec((d, half), lambda n, j: (0, jnp.maximum(2 * j - 1, 0))),
                  pl.BlockSpec((d, half), lambda n, j: (0, jnp.minimum(2 * j, 2 * n_tiles - 1))),
                  stat_spec, stat_spec,
                  pl.BlockSpec((PEER_HEADS, 8, tm), lambda n, j: (0, 0, n))],
        out_specs=pl.BlockSpec((d, tm), lambda n, j: (0, n)),
        out_shape=jax.ShapeDtypeStruct((d, s), F32),
        scratch_shapes=[pltpu.VMEM((half, tm), F32), pltpu.VMEM((half, tm), F32)],
        compiler_params=_cparams(("parallel", "arbitrary")),
        name="peer_dense",
    )(xn, down, up_t, up_t, s1, s2, rows)


def _add_t_kernel(a_ref, bt_ref, o_ref):
    for c in range(a_ref.shape[1] // LANES):
        sl = slice(c * LANES, (c + 1) * LANES)
        o_ref[:, sl] = a_ref[:, sl] + bt_ref[sl, :].T


def _add_transposed(a, b_t, tm=256):
    s, d = a.shape
    spec = pl.BlockSpec((tm, d), lambda i: (i, 0))
    return pl.pallas_call(
        _add_t_kernel, grid=(s // tm,),
        in_specs=[spec, pl.BlockSpec((d, tm), lambda i: (0, i))], out_specs=spec,
        out_shape=jax.ShapeDtypeStruct((s, d), F32),
        compiler_params=_cparams(("parallel",)), name="residual_add",
    )(a, b_t)


def _tile(n, pref):
    t = pref
    while n % t:
        t //= 2
    return t


def _layer(x, attn_norm_g, w_in, qk_gain_a, qk_gain_b, rel_bias, cmp_pos, cmp_w1, cmp_w2,
           out_norm_g, w_out, ffn_norm_g, peer_w_query, peer_sub_keys, peer_down, peer_up):
    s, d = x.shape
    scale = HEAD_DIM ** -0.5
    ones = jnp.ones((HEAD_DIM,), F32)
    zeros = jnp.zeros((HEAD_DIM,), F32)

    xn = _rmsnorm(x, attn_norm_g)
    q_scale = scale * LOG2E
    gain_a = jnp.concatenate([qk_gain_a[0] * q_scale] * 16 + [qk_gain_a[1]] * 16 + [ones] * 16)
    mode_a = jnp.concatenate([ones] * 32 + [zeros] * 16)
    gain_b = jnp.concatenate([qk_gain_b[0] * q_scale] * 16 + [ones] * 8 + [qk_gain_b[2]] * 4 + [ones] * 4
                             + [qk_gain_b[3]] * 4 + [ones] * 4)
    mode_b = jnp.concatenate([ones] * 16 + [zeros] * 8 + [ones] * 4 + [zeros] * 4 + [ones] * 4 + [zeros] * 4)
    n_gate = w_in.shape[1] - PROJ_COLS
    w_gate = jnp.pad(w_in[:, PROJ_COLS:], ((0, 0), (0, LANES - n_gate))).astype(BF16)
    tm = _tile(s, 1024)
    dilations = tuple(dil for _, dil in DILATED_PATTERNS)
    proj_a = _proj_headnorm(xn, w_in[:, :PROJ_A_COLS].astype(BF16), gain_a.astype(F32).reshape(1, -1),
                            mode_a.reshape(1, -1), tm, 512, dilations)
    proj = _proj_headnorm(xn, w_in[:, PROJ_A_COLS:PROJ_COLS].astype(BF16), gain_b.astype(F32).reshape(1, -1),
                          mode_b.reshape(1, -1), tm, 1024)[0]
    gl = _matmul(xn, w_gate, F32, tm, LANES)

    table_a = rel_bias[:N_HEADS_A] * LOG2E
    table_b = rel_bias[N_HEADS_A:] * LOG2E

    o_dil, lse_dil = [], []
    hb = 8
    for (window, dil), arr in zip(DILATED_PATTERNS, proj_a):
        arr = arr.reshape(dil, s // dil, PROJ_A_COLS)
        bias = _band_bias(table_a, window // dil, dil, 1).reshape(N_HEADS_A // hb, hb, BAND_TQ, 2 * BAND_TQ)
        o, lse = _banded_attention(
            arr, bias, n_hb=N_HEADS_A // hb, hb=hb, r=1, nprev=1,
            qcol=lambda hh: COL_QA // hb + hh, kcol=lambda hh: COL_KA // hb + hh, vcol=lambda hh: COL_VA // hb + hh,
            with_lse=True)
        o_dil.append(o.reshape(s, N_HEADS_A * HEAD_DIM))
        lse_dil.append(lse.reshape(s, (N_HEADS_A // hb) * LANES))

    nprev_w = -(-(WIN_B - 1) // BAND_TQ)
    gw = 2
    bias_w = _band_bias(table_b, WIN_B - 1, 1, nprev_w).reshape(
        N_KV_B // gw, gw * Q_PER_KV_B, BAND_TQ, (nprev_w + 1) * BAND_TQ)
    o_win = _banded_attention(
        proj[None], bias_w, n_hb=N_KV_B // gw, hb=gw * Q_PER_KV_B, r=Q_PER_KV_B, nprev=nprev_w,
        qcol=lambda hh: COL_QB // (gw * Q_PER_KV_B) + hh, kcol=lambda hh: COL_KWIN // gw + hh,
        vcol=lambda hh: COL_VWIN // gw + hh, with_lse=False)[0]

    n_chunk = s // CMP_STRIDE
    n_cmp = (s - CMP_BLOCK) // CMP_STRIDE + 1
    raw = proj[:, COL_KCMP * HEAD_DIM:(COL_VCMP + N_KV_B) * HEAD_DIM]
    xt = raw.reshape(n_chunk, CMP_STRIDE, 2, N_KV_B, HEAD_DIM).transpose(2, 3, 1, 0, 4)
    kv_c = _compress(xt, cmp_w1.astype(BF16), cmp_w2.astype(BF16),
                     cmp_pos.reshape(2, 1, CMP_BLOCK * HEAD_DIM).astype(BF16),
                     qk_gain_b[1].reshape(1, HEAD_DIM).astype(F32))
    n_rows = CMP_PAD + n_chunk + 16
    kv_c = jnp.pad(kv_c[:, :, :n_cmp], ((0, 0), (0, 0), (CMP_PAD, n_rows - CMP_PAD - n_cmp), (0, 0)))
    kcp = kv_c[0]
    vcpt = kv_c[1].transpose(0, 2, 1)

    tl = np.arange(CMP_TQ)[None, :]
    il = np.arange(CMP_BAND)[:, None]
    rel_near = CMP_STRIDE * CMP_PAD - (CMP_BLOCK - 1) + tl - CMP_STRIDE * il
    bnear = _bias_of_rel(table_b, rel_near).reshape(N_KV_B, Q_PER_KV_B, CMP_BAND, CMP_TQ)
    bnear = bnear.transpose(0, 2, 1, 3).reshape(N_KV_B, CMP_BAND, Q_PER_KV_B * CMP_TQ)
    b31 = table_b[:, N_BUCKETS - 1].astype(F32)
    far_hi = b31.astype(BF16).astype(F32)
    far_lo = (b31 - far_hi).astype(BF16).astype(F32)
    bfar = jnp.stack([jnp.repeat(v.reshape(N_KV_B, Q_PER_KV_B), CMP_TQ, axis=1) for v in (far_hi, far_lo)], axis=1)
    bfar = jnp.pad(bfar, ((0, 0), (0, 6), (0, 0)))
    caug_np = np.zeros((n_rows, CMP_AUG), np.float32)
    caug_np[:, 0:2] = 1.0
    caug_np[:CMP_PAD, 2] = 1.0
    caug_np[np.arange(n_rows), 3 + np.arange(n_rows) // 16] = 1.0
    kc_aug = jnp.concatenate([kcp, jnp.broadcast_to(jnp.asarray(caug_np, BF16)[None], (N_KV_B, n_rows, CMP_AUG))],
                             axis=2)
    n_slc = s // SLC_BLOCK
    ratio = SLC_BLOCK // CMP_STRIDE
    tt_np = np.zeros((n_slc, n_rows), np.float32)
    for jblk in range(n_slc):
        for off, wgt in ((-1, 1.0), (0, 2.0), (1, 2.0), (2, 2.0), (3, 1.0)):
            i_c = ratio * jblk + off
            if 0 <= i_c < n_cmp:
                tt_np[jblk, CMP_PAD + i_c] = wgt
    tt = jnp.asarray(tt_np, BF16)
    o_cmp, mask = _cmp_select(proj, kc_aug, vcpt, bnear, bfar, tt)

    vt = proj[:, COL_VSLC * HEAD_DIM:(COL_VSLC + N_KV_B) * HEAD_DIM].reshape(s, N_KV_B, HEAD_DIM).transpose(1, 2, 0)
    vt = jnp.concatenate([vt, jnp.ones((N_KV_B, 16, s), BF16)], axis=1)
    kl = np.arange(CMP_TQ)[:, None]
    od = np.arange(SLC_NEAR)[:, None, None]
    rel_s = CMP_TQ * od + tl[None] - kl[None]
    btile = _bias_of_rel(table_b, rel_s).reshape(N_KV_B, Q_PER_KV_B, SLC_NEAR, CMP_TQ, CMP_TQ)
    bdiff = btile - (far_hi + far_lo).reshape(N_KV_B, Q_PER_KV_B, 1, 1, 1)
    bdiff = bdiff.transpose(0, 2, 3, 1, 4).reshape(N_KV_B, SLC_NEAR, CMP_TQ, Q_PER_KV_B * CMP_TQ)
    aug_np = np.zeros((s, SLC_AUG), np.float32)
    aug_np[np.arange(s), (np.arange(s) % CMP_TQ) // SLC_BLOCK] = 1.0
    aug_np[:, CMP_TQ // SLC_BLOCK:CMP_TQ // SLC_BLOCK + 2] = 1.0
    k_slc = proj[:, COL_KSLC * HEAD_DIM:(COL_KSLC + N_KV_B) * HEAD_DIM].reshape(s, N_KV_B, HEAD_DIM)
    k_aug = jnp.concatenate([k_slc.transpose(1, 0, 2),
                             jnp.broadcast_to(jnp.asarray(aug_np, BF16)[None], (N_KV_B, s, SLC_AUG))], axis=2)
    o_slc = _slc_attention(proj, k_aug, vt, mask, bdiff, bfar)

    o_n = _combine(o_dil, lse_dil, o_cmp, o_slc, o_win, gl, out_norm_g.astype(F32))
    h = _matmul_residual(o_n, w_out.astype(BF16), x, tm, _tile(d, 1024))

    xn2 = _rmsnorm(h, ffn_norm_g)
    qp = _matmul(xn2, peer_w_query.astype(BF16), BF16, tm, 1024)
    s1, s2, rows = _peer_topk(qp, peer_sub_keys.astype(BF16))
    peer_t = _peer_dense(xn2, peer_down.astype(BF16), peer_up.T.astype(BF16), s1, s2, rows, _tile(s, 512), 512)
    return _add_transposed(h, peer_t)


def kernel(x, attn_norm_g, w_in, qk_gain_a, qk_gain_b, rel_bias, cmp_pos, cmp_w1, cmp_w2, out_norm_g, w_out,
           ffn_norm_g, peer_w_query, peer_sub_keys, peer_down, peer_up):
    b = x.shape[0]
    outs = [_layer(x[bi], attn_norm_g[0], w_in[0], qk_gain_a[0], qk_gain_b[0], rel_bias, cmp_pos[0], cmp_w1[0],
                   cmp_w2[0], out_norm_g[0], w_out[0], ffn_norm_g[0], peer_w_query[0], peer_sub_keys[0],
                   peer_down[0], peer_up[0]) for bi in range(b)]
    return jnp.stack(outs, axis=0)
```

```python
import functools
import math

import jax
import jax.numpy as jnp
import numpy as np
from jax import lax
from jax.experimental import pallas as pl
from jax.experimental.pallas import tpu as pltpu

F32 = jnp.float32
BF16 = jnp.bfloat16

HEAD_DIM = 128
LANES = 128
N_HEADS_A = 16
N_HEADS_B = 16
N_KV_B = 4
Q_PER_KV_B = 4
DILATED_PATTERNS = ((128, 1), (512, 4), (2048, 16))
CMP_BLOCK = 32
CMP_STRIDE = 16
SLC_BLOCK = 64
N_SLC = 16
WIN_B = 512
N_BUCKETS = 32
MAX_DISTANCE = 2048
PEER_HEADS = 8
N_KEYS = 128
PEER_TOPK = 16
RMS_EPS = 1e-6
NEG = -1e30
BIG = 3e38
LOG2E = math.log2(math.e)

COL_QA, COL_KA, COL_VA = 0, 16, 32
N_PROJ_A_HEADS = 48
COL_QB, COL_KCMP, COL_VCMP, COL_KSLC, COL_VSLC, COL_KWIN, COL_VWIN = 0, 16, 20, 24, 28, 32, 36
N_PROJ_B_HEADS = 40
PROJ_A_COLS = N_PROJ_A_HEADS * HEAD_DIM
PROJ_COLS = (N_PROJ_A_HEADS + N_PROJ_B_HEADS) * HEAD_DIM

BAND_TQ = 128
CMP_TQ = 256
CMP_PAD = 112
CMP_BAND = 128
SLC_NEAR = 8
SLC_QT = 2
SLC_AUG = 16
CMP_AUG = 128
VMEM_LIMIT = 56 * 1024 * 1024


def _cparams(sem):
    return pltpu.CompilerParams(dimension_semantics=sem, vmem_limit_bytes=VMEM_LIMIT)


def _rel_bucket(dist):
    n = np.maximum(dist, 0)
    max_exact = N_BUCKETS // 2
    nf = np.maximum(n, 1).astype(np.float32)
    log_part = (np.log(nf / np.float32(max_exact)) / np.float32(math.log(MAX_DISTANCE / max_exact))
                * np.float32(N_BUCKETS - max_exact))
    large = np.minimum(max_exact + log_part.astype(np.int32), N_BUCKETS - 1)
    return np.where(n < max_exact, n, large).astype(np.int32)


def _bias_of_rel(table, rel):
    rel = np.asarray(rel)
    bucket = np.where(rel >= 0, _rel_bucket(rel), N_BUCKETS).reshape(-1)
    onehot = (jnp.asarray(bucket)[None, :] == jnp.arange(N_BUCKETS + 1)[:, None]).astype(F32)
    table = jnp.concatenate([table.astype(F32), jnp.full((table.shape[0], 1), NEG, F32)], axis=1)
    out = jnp.dot(table, onehot, precision=lax.Precision.HIGHEST)
    return out.reshape((table.shape[0],) + rel.shape)


def _rmsnorm_kernel(x_ref, g_ref, o_ref):
    x = x_ref[...]
    ms = jnp.mean(x * x, axis=-1, keepdims=True)
    o_ref[...] = (x * lax.rsqrt(ms + RMS_EPS) * g_ref[...]).astype(o_ref.dtype)


def _rmsnorm(x, g, tm=256):
    s, d = x.shape
    return pl.pallas_call(
        _rmsnorm_kernel,
        grid=(s // tm,),
        in_specs=[pl.BlockSpec((tm, d), lambda i: (i, 0)), pl.BlockSpec((1, d), lambda i: (0, 0))],
        out_specs=pl.BlockSpec((tm, d), lambda i: (i, 0)),
        out_shape=jax.ShapeDtypeStruct((s, d), BF16),
        compiler_params=_cparams(("parallel",)),
        name="rmsnorm",
    )(x, g.reshape(1, d).astype(F32))


def _proj_kernel(x_ref, w_ref, gain_ref, mode_ref, *refs, n_chunks, dilations, tm):
    outs = refs[:len(dilations)]
    acc = jnp.dot(x_ref[...], w_ref[...], preferred_element_type=F32)
    for c in range(n_chunks):
        sl = slice(c * LANES, (c + 1) * LANES)
        t = acc[:, sl]
        ms = jnp.mean(t * t, axis=-1, keepdims=True)
        mult = jnp.where(mode_ref[:, sl] > 0, lax.rsqrt(ms + RMS_EPS) * gain_ref[:, sl], 1.0)
        y = t * mult
        if len(dilations) > 1:
            y_scr = refs[len(dilations)]
            y_scr[c] = y
        for o_ref, dil in zip(outs, dilations):
            if dil == 1:
                o_ref[:, sl] = y.astype(o_ref.dtype)
            else:
                for r in range(dil):
                    o_ref[r, :, sl] = y_scr[c, pl.ds(r, tm // dil, stride=dil), :].astype(o_ref.dtype)


def _proj_headnorm(xn, w, gain, mode, tm, tn, dilations=(1,)):
    s, d = xn.shape
    n = w.shape[1]
    out_specs, out_shape = [], []
    for dil in dilations:
        if dil == 1:
            out_specs.append(pl.BlockSpec((tm, tn), lambda i, j: (i, j)))
            out_shape.append(jax.ShapeDtypeStruct((s, n), BF16))
        else:
            out_specs.append(pl.BlockSpec((dil, tm // dil, tn), lambda i, j: (0, i, j)))
            out_shape.append(jax.ShapeDtypeStruct((dil, s // dil, n), BF16))
    scratch = [pltpu.VMEM((tn // LANES, tm, LANES), F32)] if len(dilations) > 1 else []
    return pl.pallas_call(
        functools.partial(_proj_kernel, n_chunks=tn // LANES, dilations=tuple(dilations), tm=tm),
        grid=(s // tm, n // tn),
        in_specs=[pl.BlockSpec((tm, d), lambda i, j: (i, 0)),
                  pl.BlockSpec((d, tn), lambda i, j: (0, j)),
                  pl.BlockSpec((1, tn), lambda i, j: (0, j)),
                  pl.BlockSpec((1, tn), lambda i, j: (0, j))],
        out_specs=out_specs,
        out_shape=out_shape,
        scratch_shapes=scratch,
        compiler_params=_cparams(("parallel", "arbitrary")),
        name="proj_headnorm",
    )(xn, w, gain, mode)


def _matmul_kernel(x_ref, w_ref, o_ref):
    o_ref[...] = jnp.dot(x_ref[...], w_ref[...], preferred_element_type=F32).astype(o_ref.dtype)


def _matmul(x, w, out_dtype, tm, tn):
    s, d = x.shape
    n = w.shape[1]
    return pl.pallas_call(
        _matmul_kernel,
        grid=(s // tm, n // tn),
        in_specs=[pl.BlockSpec((tm, d), lambda i, j: (i, 0)), pl.BlockSpec((d, tn), lambda i, j: (0, j))],
        out_specs=pl.BlockSpec((tm, tn), lambda i, j: (i, j)),
        out_shape=jax.ShapeDtypeStruct((s, n), out_dtype),
        compiler_params=_cparams(("parallel", "arbitrary")),
        name="matmul",
    )(x, w)


def _matmul_res_kernel(x_ref, w_ref, r_ref, o_ref):
    o_ref[...] = r_ref[...] + jnp.dot(x_ref[...], w_ref[...], preferred_element_type=F32)


def _matmul_residual(x, w, res, tm, tn):
    s, d = x.shape
    n = w.shape[1]
    return pl.pallas_call(
        _matmul_res_kernel,
        grid=(s // tm, n // tn),
        in_specs=[pl.BlockSpec((tm, d), lambda i, j: (i, 0)),
                  pl.BlockSpec((d, tn), lambda i, j: (0, j)),
                  pl.BlockSpec((tm, tn), lambda i, j: (i, j))],
        out_specs=pl.BlockSpec((tm, tn), lambda i, j: (i, j)),
        out_shape=jax.ShapeDtypeStruct((s, n), F32),
        compiler_params=_cparams(("parallel", "arbitrary")),
        name="matmul_residual",
    )(x, w, res)


def _banded_kernel(*refs, hb, r, nprev, tq, with_lse):
    nk = nprev + 1
    q_ref = refs[0]
    k_refs = refs[1:1 + nk]
    v_refs = refs[1 + nk:1 + 2 * nk]
    bias_ref = refs[1 + 2 * nk]
    o_ref = refs[2 + 2 * nk]
    i = pl.program_id(2)
    span = nk * tq
    ng = hb // r
    q3 = jnp.stack([jnp.concatenate([q_ref[:, (g * r + rr) * LANES:(g * r + rr + 1) * LANES] for rr in range(r)],
                                    axis=0) for g in range(ng)], axis=0)
    parts = []
    for j in range(nk):
        k3 = jnp.stack([k_refs[j][:, g * LANES:(g + 1) * LANES] for g in range(ng)], axis=0)
        sj = jnp.einsum('gqd,gkd->gqk', q3, k3, preferred_element_type=F32)
        if j < nprev:
            sj = sj + jnp.where(i >= nprev - j, 0.0, NEG)
        parts.append(sj)
    s = jnp.concatenate(parts, axis=2) + bias_ref[...].reshape(ng, r * tq, span)
    m = jnp.max(s, axis=-1, keepdims=True)
    p = jnp.exp2(s - m)
    den = jnp.sum(p, axis=-1, keepdims=True)
    pb = p.astype(BF16)
    o = None
    for j in range(nk):
        v3 = jnp.stack([v_refs[j][:, g * LANES:(g + 1) * LANES] for g in range(ng)], axis=0)
        oj = jnp.einsum('gqk,gkd->gqd', pb[:, :, j * tq:(j + 1) * tq], v3, preferred_element_type=F32)
        o = oj if o is None else o + oj
    o = o / den
    for g in range(ng):
        for rr in range(r):
            h = g * r + rr
            o_ref[:, h * LANES:(h + 1) * LANES] = o[g, rr * tq:(rr + 1) * tq].astype(o_ref.dtype)
    if with_lse:
        lse_ref = refs[3 + 2 * nk]
        lse = m + jnp.log(den) * LOG2E
        lane = lax.broadcasted_iota(jnp.int32, (tq, LANES), 1)
        lse_mat = jnp.zeros((tq, LANES), F32)
        for g in range(ng):
            lse_mat = jnp.where(lane == g, lse[g], lse_mat)
        lse_ref[...] = lse_mat


def _banded_attention(arr, bias, *, n_hb, hb, r, nprev, qcol, kcol, vcol, with_lse):
    tq = BAND_TQ
    n_r, length = arr.shape[0], arr.shape[1]
    nk = nprev + 1
    qw = hb * LANES
    kw = (hb // r) * LANES

    def kmap(col, back):
        return lambda rr, hh, i: (rr, jnp.maximum(i - back, 0), col(hh))

    in_specs = [pl.BlockSpec((None, tq, qw), lambda rr, hh, i: (rr, i, qcol(hh)))]
    in_specs += [pl.BlockSpec((None, tq, kw), kmap(kcol, nprev - j)) for j in range(nk)]
    in_specs += [pl.BlockSpec((None, tq, kw), kmap(vcol, nprev - j)) for j in range(nk)]
    in_specs += [pl.BlockSpec((None, hb, tq, nk * tq), lambda rr, hh, i: (hh, 0, 0, 0))]
    out_specs = [pl.BlockSpec((None, tq, qw), lambda rr, hh, i: (rr, i, hh))]
    out_shape = [jax.ShapeDtypeStruct((n_r, length, n_hb * qw), BF16)]
    if with_lse:
        out_specs.append(pl.BlockSpec((None, tq, LANES), lambda rr, hh, i: (rr, i, hh)))
        out_shape.append(jax.ShapeDtypeStruct((n_r, length, n_hb * LANES), F32))
    return pl.pallas_call(
        functools.partial(_banded_kernel, hb=hb, r=r, nprev=nprev, tq=tq, with_lse=with_lse),
        grid=(n_r, n_hb, length // tq),
        in_specs=in_specs,
        out_specs=out_specs,
        out_shape=out_shape,
        compiler_params=_cparams(("parallel", "parallel", "arbitrary")),
        name="banded_attention",
    )(*([arr] * (1 + 2 * nk)), bias)


def _band_bias(table, max_dist, dist_scale, nprev):
    tq = BAND_TQ
    q_loc = np.arange(tq)[:, None]
    k_loc = np.arange((nprev + 1) * tq)[None, :] - nprev * tq
    rel = q_loc - k_loc
    return _bias_of_rel(table, np.where((rel >= 0) & (rel <= max_dist), rel * dist_scale, -1))


def _compress_kernel(x_ref, w1_ref, w2_ref, pos_ref, gain_ref, o_ref, *, n_chunk):
    half = CMP_STRIDE
    a = jnp.zeros((n_chunk, HEAD_DIM), F32)
    b = jnp.zeros((n_chunk, HEAD_DIM), F32)
    for c in range(half):
        xc = x_ref[c]
        a = a + jnp.dot(xc, w1_ref[c * HEAD_DIM:(c + 1) * HEAD_DIM, :], preferred_element_type=F32)
        b = b + jnp.dot(xc, w1_ref[(half + c) * HEAD_DIM:(half + c + 1) * HEAD_DIM, :],
                        preferred_element_type=F32)
    pos = jnp.broadcast_to(pos_ref[...], (8, CMP_BLOCK * HEAD_DIM))
    posterm = jnp.dot(pos, w1_ref[...], preferred_element_type=F32)[0:1, :]
    pre = a + pltpu.roll(b, n_chunk - 1, 0) + posterm
    hid = jax.nn.gelu(pre)
    out = jnp.dot(hid.astype(BF16), w2_ref[...], preferred_element_type=F32)
    ms = jnp.mean(out * out, axis=-1, keepdims=True)
    normed = out * lax.rsqrt(ms + RMS_EPS) * gain_ref[...]
    o_ref[...] = jnp.where(pl.program_id(0) == 0, normed, out).astype(o_ref.dtype)


def _compress(xt, w1, w2, pos, gain):
    n_chunk = xt.shape[3]
    return pl.pallas_call(
        functools.partial(_compress_kernel, n_chunk=n_chunk),
        grid=(2, N_KV_B),
        in_specs=[pl.BlockSpec((None, None, CMP_STRIDE, n_chunk, HEAD_DIM), lambda w, g: (w, g, 0, 0, 0)),
                  pl.BlockSpec((None, CMP_BLOCK * HEAD_DIM, HEAD_DIM), lambda w, g: (w, 0, 0)),
                  pl.BlockSpec((None, HEAD_DIM, HEAD_DIM), lambda w, g: (w, 0, 0)),
                  pl.BlockSpec((None, 1, CMP_BLOCK * HEAD_DIM), lambda w, g: (w, 0, 0)),
                  pl.BlockSpec((1, HEAD_DIM), lambda w, g: (0, 0))],
        out_specs=pl.BlockSpec((None, None, n_chunk, HEAD_DIM), lambda w, g: (w, g, 0, 0)),
        out_shape=jax.ShapeDtypeStruct((2, N_KV_B, n_chunk, HEAD_DIM), BF16),
        compiler_params=_cparams(("parallel", "parallel")),
        name="nsa_compress",
    )(xt, w1, w2, pos, gain)


def _cmp_select_kernel(q_ref, kc_ref, vct_ref, bnear_ref, bfar_ref, tt_ref, o_ref, mask_ref, p_scr, *, tq):
    n = pl.program_id(1)
    t0 = n * tq
    wide = Q_PER_KV_B * tq
    band_lo = pl.multiple_of(n * (tq // CMP_STRIDE), 16)
    t = t0 + lax.broadcasted_iota(jnp.int32, (1, tq), 1)
    valid = jnp.concatenate([t >= CMP_BLOCK - 1] * Q_PER_KV_B, axis=1)
    q_t = jnp.concatenate([q_ref[:, r * LANES:(r + 1) * LANES].astype(F32).T.astype(BF16)
                           for r in range(Q_PER_KV_B)], axis=1)
    ri = lax.broadcasted_iota(jnp.int32, (CMP_AUG, wide), 0)
    blocked = (ri == 2) | ((ri >= 3) & (ri - 3 >= n * (tq // CMP_STRIDE) // 16))
    aug = jnp.where(ri == 0, bfar_ref[0:1, :], jnp.where(ri == 1, bfar_ref[1:2, :],
                                                         jnp.where(blocked, NEG, 0.0)))
    rhs = jnp.concatenate([q_t, aug.astype(BF16)], axis=0)
    s_far = jnp.dot(kc_ref[...], rhs, preferred_element_type=F32)
    brow = lax.broadcasted_iota(jnp.int32, (CMP_BAND, wide), 0) + band_lo
    s_band = jnp.dot(kc_ref[pl.ds(band_lo, CMP_BAND), 0:HEAD_DIM], q_t, preferred_element_type=F32)
    s_band = s_band + bnear_ref[...] + jnp.where(brow >= CMP_PAD, 0.0, NEG)
    m = jnp.maximum(jnp.max(s_far, axis=0, keepdims=True), jnp.max(s_band, axis=0, keepdims=True))
    p_far = jnp.exp2(s_far - m)
    p_band = jnp.exp2(s_band - m)
    den = jnp.sum(p_far, axis=0, keepdims=True) + jnp.sum(p_band, axis=0, keepdims=True)
    inv = jnp.where(valid, 1.0 / den, 0.0)
    p_scr[...] = p_far * inv
    p_scr[pl.ds(band_lo, CMP_BAND), :] = p_band * inv
    p = p_scr[...]
    o_t = jnp.dot(vct_ref[...], p.astype(BF16), preferred_element_type=F32)
    psum = p[:, 0:tq]
    for r in range(Q_PER_KV_B):
        o_ref[:, r * LANES:(r + 1) * LANES] = o_t[:, r * tq:(r + 1) * tq].T.astype(o_ref.dtype)
        if r:
            psum = psum + p[:, r * tq:(r + 1) * tq]
    p_hi = psum.astype(BF16)
    p_lo = (psum - p_hi.astype(F32)).astype(BF16)
    imp = (jnp.dot(tt_ref[...], p_hi, preferred_element_type=F32)
           + jnp.dot(tt_ref[...], p_lo, preferred_element_type=F32))
    n_slc = imp.shape[0]
    jj = lax.broadcasted_iota(jnp.int32, (n_slc, tq), 0)
    cur = t // SLC_BLOCK
    allowed = jj <= cur
    score = jnp.where(jj == 0, 3e9,
                      jnp.where(jj == cur, 2e9,
                                jnp.where(jj == cur - 1, 1e9, jnp.where(allowed, imp, NEG))))
    work = score
    thr = jnp.zeros((1, tq), F32)
    for _ in range(N_SLC):
        thr = jnp.max(work, axis=0, keepdims=True)
        work = jnp.where(work >= thr, -BIG, work)
    mask_ref[...] = jnp.where((score >= thr) & allowed, 0.0, NEG)


def _cmp_select(proj, kc_aug, vcpt, bnear, bfar, tt):
    s = proj.shape[0]
    tq = CMP_TQ
    wide = Q_PER_KV_B * tq
    n_rows = kc_aug.shape[1]
    n_slc = tt.shape[0]
    return pl.pallas_call(
        functools.partial(_cmp_select_kernel, tq=tq),
        grid=(N_KV_B, s // tq),
        in_specs=[pl.BlockSpec((tq, Q_PER_KV_B * LANES), lambda g, n: (n, COL_QB // Q_PER_KV_B + g)),
                  pl.BlockSpec((None, n_rows, HEAD_DIM + CMP_AUG), lambda g, n: (g, 0, 0)),
                  pl.BlockSpec((None, HEAD_DIM, n_rows), lambda g, n: (g, 0, 0)),
                  pl.BlockSpec((None, CMP_BAND, wide), lambda g, n: (g, 0, 0)),
                  pl.BlockSpec((None, 8, wide), lambda g, n: (g, 0, 0)),
                  pl.BlockSpec((n_slc, n_rows), lambda g, n: (0, 0))],
        out_specs=[pl.BlockSpec((tq, Q_PER_KV_B * LANES), lambda g, n: (n, g)),
                   pl.BlockSpec((None, n_slc, tq), lambda g, n: (g, 0, n))],
        out_shape=[jax.ShapeDtypeStruct((s, N_HEADS_B * HEAD_DIM), BF16),
                   jax.ShapeDtypeStruct((N_KV_B, n_slc, s), F32)],
        scratch_shapes=[pltpu.VMEM((n_rows, wide), F32)],
        compiler_params=_cparams(("parallel", "arbitrary")),
        name="nsa_cmp_select",
    )(proj, kc_aug, vcpt, bnear, bfar, tt)


def _slc_kernel(q_ref, k_ref, vt_ref, mask_ref, bdiff_ref, bfar_ref, o_ref,
                s_scr, p_scr, alpha_scr, m_scr, acc_scr, *, tq):
    qt = SLC_QT
    first = pl.program_id(1) * qt
    last = first + qt - 1
    tk = tq
    blocks_per_chunk = tk // SLC_BLOCK
    tile_w = Q_PER_KV_B * tq
    wide = qt * tile_w
    q_t = jnp.concatenate([q_ref[t * tq:(t + 1) * tq, r * LANES:(r + 1) * LANES].astype(F32).T.astype(BF16)
                           for t in range(qt) for r in range(Q_PER_KV_B)], axis=1)
    tail_rows = [jnp.concatenate([bfar_ref[0:1, :]] * qt, axis=1), jnp.concatenate([bfar_ref[1:2, :]] * qt, axis=1),
                 jnp.zeros((SLC_AUG - blocks_per_chunk - 2, wide), F32)]

    def scores(c, slot, near):
        pens = [jnp.where(c > first + t, NEG, 0.0) for t in range(qt)]
        c = jnp.minimum(c, last)
        k0 = pl.multiple_of(c * tk, tk)
        mrows = []
        for b in range(blocks_per_chunk):
            row = mask_ref[pl.ds(c * blocks_per_chunk + b, 1), :]
            mrows.append(jnp.concatenate([row[:, t * tq:(t + 1) * tq] + pens[t]
                                          for t in range(qt) for _ in range(Q_PER_KV_B)], axis=1))
        aug = jnp.concatenate(mrows + tail_rows, axis=0).astype(BF16)
        rhs = jnp.concatenate([q_t, aug], axis=0)
        s = jnp.dot(k_ref[pl.ds(k0, tk), :], rhs, preferred_element_type=F32)
        if near:
            s = jnp.concatenate(
                [s[:, t * tile_w:(t + 1) * tile_w] + bdiff_ref[jnp.clip(first + t - c, 0, SLC_NEAR - 1)]
                 for t in range(qt)], axis=1)
        s_scr[slot] = s

    def softmax_update(slot):
        s = s_scr[slot]
        m_old = m_scr[...]
        m_new = jnp.maximum(m_old, jnp.max(s, axis=0, keepdims=True))
        alpha = jnp.exp2(m_old - m_new)
        p = jnp.exp2(s - m_new)
        m_scr[...] = m_new
        alpha_scr[slot] = alpha
        p_scr[slot] = p.astype(BF16)

    def values(c, slot):
        k0 = pl.multiple_of(jnp.clip(c, 0, last) * tk, tk)
        acc_scr[...] = alpha_scr[slot] * acc_scr[...] + jnp.dot(
            vt_ref[:, pl.ds(k0, tk)], p_scr[slot], preferred_element_type=F32)

    m_scr[...] = jnp.full(m_scr.shape, -BIG, F32)
    acc_scr[...] = jnp.zeros(acc_scr.shape, F32)
    p_scr[1] = jnp.zeros(p_scr.shape[1:], BF16)
    alpha_scr[1] = jnp.ones(alpha_scr.shape[1:], F32)
    scores(0, 0, True)

    def body(i, near):
        c = 2 * i
        scores(c + 1, 1, near)
        softmax_update(0)
        values(c - 1, 1)
        scores(c + 2, 0, near)
        softmax_update(1)
        values(c, 0)

    def far_body(i, carry):
        body(i, False)
        return carry

    def near_body(i, carry):
        body(i, True)
        return carry

    n_pairs = (last + 2) // 2
    n_far = jnp.clip((first - (SLC_NEAR - 1)) // 2, 0, n_pairs)
    lax.fori_loop(0, n_far, far_body, 0)
    lax.fori_loop(n_far, n_pairs, near_body, 0)
    values(2 * n_pairs - 1, 1)
    o_t = acc_scr[0:HEAD_DIM, :] / acc_scr[HEAD_DIM:HEAD_DIM + 1, :]
    for t in range(qt):
        for r in range(Q_PER_KV_B):
            lanes = slice((t * Q_PER_KV_B + r) * tq, (t * Q_PER_KV_B + r + 1) * tq)
            o_ref[t * tq:(t + 1) * tq, r * LANES:(r + 1) * LANES] = o_t[:, lanes].T.astype(o_ref.dtype)


def _slc_attention(proj, k_aug, vt, mask, bdiff, bfar):
    s = proj.shape[0]
    tq = CMP_TQ
    rows = SLC_QT * tq
    n_slc = mask.shape[1]
    tile_w = Q_PER_KV_B * tq
    wide = SLC_QT * tile_w
    v_rows = vt.shape[1]
    return pl.pallas_call(
        functools.partial(_slc_kernel, tq=tq),
        grid=(N_KV_B, s // rows),
        in_specs=[pl.BlockSpec((rows, Q_PER_KV_B * LANES), lambda g, n: (n, COL_QB // Q_PER_KV_B + g)),
                  pl.BlockSpec((None, s, HEAD_DIM + SLC_AUG), lambda g, n: (g, 0, 0)),
                  pl.BlockSpec((None, v_rows, s), lambda g, n: (g, 0, 0)),
                  pl.BlockSpec((None, n_slc, rows), lambda g, n: (g, 0, n)),
                  pl.BlockSpec((None, SLC_NEAR, tq, tile_w), lambda g, n: (g, 0, 0, 0)),
                  pl.BlockSpec((None, 8, tile_w), lambda g, n: (g, 0, 0))],
        out_specs=pl.BlockSpec((rows, Q_PER_KV_B * LANES), lambda g, n: (n, g)),
        out_shape=jax.ShapeDtypeStruct((s, N_HEADS_B * HEAD_DIM), BF16),
        scratch_shapes=[pltpu.VMEM((2, tq, wide), F32),
                        pltpu.VMEM((2, tq, wide), BF16),
                        pltpu.VMEM((2, 1, wide), F32),
                        pltpu.VMEM((1, wide), F32),
                        pltpu.VMEM((v_rows, wide), F32)],
        compiler_params=_cparams(("parallel", "arbitrary")),
        name="nsa_selected",
    )(proj, k_aug, vt, mask, bdiff, bfar)


def _combine_kernel(o1_ref, o2_ref, o3_ref, l1_ref, l2_ref, l3_ref, oc_ref, os_ref, ow_ref, gl_ref, g_ref, o_ref,
                    o_scr, l_scr, *, tm):
    heads_per_lse_block = 8
    n_lse_blocks = N_HEADS_A // heads_per_lse_block
    dils = [dil for _, dil in DILATED_PATTERNS]
    o_in = [o1_ref, o2_ref, o3_ref]
    l_in = [l1_ref, l2_ref, l3_ref]
    for pi, dil in enumerate(dils):
        if dil == 1:
            continue
        rows = tm // dil
        for r in range(dil):
            for h in range(N_HEADS_A):
                o_scr[pi, h, pl.ds(r, rows, stride=dil), :] = o_in[pi][r, :, h * LANES:(h + 1) * LANES].astype(F32)
            for b in range(n_lse_blocks):
                l_scr[pi, b, pl.ds(r, rows, stride=dil), :] = l_in[pi][r, :, b * LANES:(b + 1) * LANES]

    def o_of(pi, h):
        if dils[pi] == 1:
            return o_in[pi][:, h * LANES:(h + 1) * LANES].astype(F32)
        return o_scr[pi, h]

    def l_of(pi, h):
        b, lane = h // heads_per_lse_block, h % heads_per_lse_block
        if dils[pi] == 1:
            return l_in[pi][:, b * LANES + lane:b * LANES + lane + 1]
        return l_scr[pi, b, :, lane:lane + 1]

    for h in range(N_HEADS_A):
        sl = slice(h * LANES, (h + 1) * LANES)
        l1, l2, l3 = l_of(0, h), l_of(1, h), l_of(2, h)
        m = jnp.maximum(jnp.maximum(l1, l2), l3)
        e1, e2, e3 = jnp.exp2(l1 - m), jnp.exp2(l2 - m), jnp.exp2(l3 - m)
        den = e1 + e2 + e3
        o = o_of(0, h) * (e1 / den) + o_of(1, h) * (e2 / den) + o_of(2, h) * (e3 / den)
        ms = jnp.mean(o * o, axis=-1, keepdims=True)
        o_ref[:, sl] = (o * lax.rsqrt(ms + RMS_EPS) * g_ref[h:h + 1, :]).astype(o_ref.dtype)
    gates = jax.nn.sigmoid(gl_ref[...])
    for h in range(N_HEADS_B):
        sl = slice(h * LANES, (h + 1) * LANES)
        o = (gates[:, 3 * h:3 * h + 1] * oc_ref[:, sl].astype(F32)
             + gates[:, 3 * h + 1:3 * h + 2] * os_ref[:, sl].astype(F32)
             + gates[:, 3 * h + 2:3 * h + 3] * ow_ref[:, sl].astype(F32))
        ms = jnp.mean(o * o, axis=-1, keepdims=True)
        hh = N_HEADS_A + h
        o_ref[:, hh * LANES:(hh + 1) * LANES] = (
            o * lax.rsqrt(ms + RMS_EPS) * g_ref[hh:hh + 1, :]).astype(o_ref.dtype)


def _combine(o_dil, lse_dil, o_cmp, o_slc, o_win, gl, out_gain, tm=256):
    s = o_cmp.shape[0]
    wa = N_HEADS_A * HEAD_DIM
    wl = (N_HEADS_A // 8) * LANES
    wide = pl.BlockSpec((tm, wa), lambda i: (i, 0))

    def dil_spec(dil, width):
        if dil == 1:
            return pl.BlockSpec((tm, width), lambda i: (i, 0))
        return pl.BlockSpec((dil, tm // dil, width), lambda i: (0, i, 0))

    dils = [dil for _, dil in DILATED_PATTERNS]
    return pl.pallas_call(
        functools.partial(_combine_kernel, tm=tm),
        grid=(s // tm,),
        in_specs=[dil_spec(dil, wa) for dil in dils] + [dil_spec(dil, wl) for dil in dils]
        + [wide, wide, wide, pl.BlockSpec((tm, LANES), lambda i: (i, 0)),
           pl.BlockSpec((N_HEADS_A + N_HEADS_B, HEAD_DIM), lambda i: (0, 0))],
        out_specs=pl.BlockSpec((tm, 2 * wa), lambda i: (i, 0)),
        out_shape=jax.ShapeDtypeStruct((s, 2 * wa), BF16),
        scratch_shapes=[pltpu.VMEM((len(dils), N_HEADS_A, tm, LANES), F32),
                        pltpu.VMEM((len(dils), wl // LANES, tm, LANES), F32)],
        compiler_params=_cparams(("parallel",)),
        name="combine_headnorm",
    )(*o_dil, *lse_dil, o_cmp, o_slc, o_win, gl, out_gain)


def _topk_rounds(work, k):
    vals = []
    for _ in range(k):
        mx = jnp.max(work, axis=0, keepdims=True)
        vals.append(mx)
        work = jnp.where(work >= mx, -BIG, work)
    return vals


def _peer_topk_kernel(q_ref, keys_ref, s1_ref, s2_ref, rows_ref, *, tm):
    half = N_KEYS
    dn = (((1,), (1,)), ((), ()))
    s1 = lax.dot_general(keys_ref[0], q_ref[:, 0:half], dn, preferred_element_type=F32)
    s2 = lax.dot_general(keys_ref[1], q_ref[:, half:2 * half], dn, preferred_element_type=F32)
    v1 = _topk_rounds(s1, PEER_TOPK + 1)
    v2 = _topk_rounds(s2, PEER_TOPK + 1)
    v2m = jnp.concatenate(v2[:PEER_TOPK], axis=0)
    cand = jnp.concatenate([v1[a] + v2m for a in range(PEER_TOPK)], axis=0)
    top = _topk_rounds(cand, PEER_TOPK + 1)
    z = jnp.zeros((1, tm), F32)
    for tv in top[:PEER_TOPK]:
        z = z + jnp.exp(tv - top[0])
    s1_ref[...] = s1
    s2_ref[...] = s2
    next_sum = jnp.maximum(top[PEER_TOPK], jnp.maximum(v1[PEER_TOPK] + v2[0], v1[0] + v2[PEER_TOPK]))
    thr = 0.5 * (top[PEER_TOPK - 1] + next_sum)
    rows_ref[...] = jnp.concatenate([thr, v1[0], v2[0], 1.0 / z] * 2, axis=0)


ROW_THR, ROW_MAX1, ROW_MAX2, ROW_INVZ = 0, 1, 2, 3


def _peer_topk(qp, keys, tm=256):
    s = qp.shape[0]
    stat = jax.ShapeDtypeStruct((PEER_HEADS, N_KEYS, s), F32)
    stat_spec = pl.BlockSpec((None, N_KEYS, tm), lambda n, h: (h, 0, n))
    return pl.pallas_call(
        functools.partial(_peer_topk_kernel, tm=tm),
        grid=(s // tm, PEER_HEADS),
        in_specs=[pl.BlockSpec((tm, 2 * N_KEYS), lambda n, h: (n, h)),
                  pl.BlockSpec((None, 2, N_KEYS, N_KEYS), lambda n, h: (h, 0, 0, 0))],
        out_specs=[stat_spec, stat_spec, pl.BlockSpec((None, 8, tm), lambda n, h: (h, 0, n))],
        out_shape=[stat, stat, jax.ShapeDtypeStruct((PEER_HEADS, 8, s), F32)],
        compiler_params=_cparams(("parallel", "arbitrary")),
        name="peer_topk",
    )(qp, keys)


def _peer_gates(s1_ref, s2_ref, e2_scr, rows_ref, i1_base, n_i1, tm, live):
    w = [jnp.zeros((N_KEYS, tm), F32) for _ in range(n_i1)]
    for h in range(PEER_HEADS):
        thr = rows_ref[h, ROW_THR:ROW_THR + 1, :]
        max1 = rows_ref[h, ROW_MAX1:ROW_MAX1 + 1, :]
        half_inv_z = rows_ref[h, ROW_INVZ:ROW_INVZ + 1, :] * (0.5 * live)
        s2 = s2_ref[h]
        e2 = e2_scr[h]
        for ii in range(n_i1):
            s1row = s1_ref[h, pl.ds(i1_base + ii, 1), :]
            e1row = jnp.exp(s1row - max1) * half_inv_z
            w[ii] = w[ii] + jnp.where(s2 >= thr - s1row, e2, 0.0) * e1row
    return w


GELU_C0 = math.sqrt(2.0 / math.pi)
GELU_C1 = 0.044715 * GELU_C0


def _gated_gelu(w_half, x):
    return (w_half * x) * (1.0 + jnp.tanh(x * (GELU_C0 + GELU_C1 * (x * x))))


def _peer_dense_kernel(xn_ref, down_ref, upb_ref, upa_ref, s1_ref, s2_ref, rows_ref, o_ref, hida_scr, hidb_scr,
                       e2_scr, *, tm, te, n_tiles):
    j = pl.program_id(1)
    half = te // 2
    n_i1 = half // N_KEYS
    dn = (((1,), (1,)), ((), ()))

    @pl.when(j == 0)
    def _():
        o_ref[...] = jnp.zeros(o_ref.shape, F32)
        hidb_scr[...] = jnp.zeros(hidb_scr.shape, F32)
        for h in range(PEER_HEADS):
            e2_scr[h] = jnp.exp(s2_ref[h] - rows_ref[h, ROW_MAX2:ROW_MAX2 + 1, :])

    jb = jnp.maximum(j - 1, 0)
    ja = jnp.minimum(j, n_tiles - 1)
    live_a = jnp.where(j < n_tiles, 1.0, 0.0)

    def mix(w, hid):
        return jnp.concatenate([_gated_gelu(w[ii], hid[ii * N_KEYS:(ii + 1) * N_KEYS]) for ii in range(n_i1)],
                               axis=0).astype(BF16)

    wb = _peer_gates(s1_ref, s2_ref, e2_scr, rows_ref, (2 * jb + 1) * n_i1, n_i1, tm, 1.0)
    hida_scr[...] = lax.dot_general(down_ref[0:half, :], xn_ref[...], dn, preferred_element_type=F32)
    o_ref[...] += jnp.dot(upb_ref[...], mix(wb, hidb_scr[...]), preferred_element_type=F32)
    wa = _peer_gates(s1_ref, s2_ref, e2_scr, rows_ref, (2 * ja) * n_i1, n_i1, tm, live_a)
    hidb_scr[...] = lax.dot_general(down_ref[half:te, :], xn_ref[...], dn, preferred_element_type=F32)
    o_ref[...] += jnp.dot(upa_ref[...], mix(wa, hida_scr[...]), preferred_element_type=F32)


def _peer_dense(xn, down, up_t, s1, s2, rows, tm, te):
    s, d = xn.shape
    n_tiles = down.shape[0] // te
    half = te // 2
    stat_spec = pl.BlockSpec((PEER_HEADS, N_KEYS, tm), lambda n, j: (0, 0, n))
    return pl.pallas_call(
        functools.partial(_peer_dense_kernel, tm=tm, te=te, n_tiles=n_tiles),
        grid=(s // tm, n_tiles + 1),
        in_specs=[pl.BlockSpec((tm, d), lambda n, j: (n, 0)),
                  pl.BlockSpec((te, d), lambda n, j: (jnp.minimum(j, n_tiles - 1), 0)),
                  pl.BlockSpec((d, half), lambda n, j: (0, jnp.maximum(2 * j - 1, 0))),
                  pl.BlockSpec((d, half), lambda n, j: (0, jnp.minimum(2 * j, 2 * n_tiles - 1))),
                  stat_spec, stat_spec,
                  pl.BlockSpec((PEER_HEADS, 8, tm), lambda n, j: (0, 0, n))],
        out_specs=pl.BlockSpec((d, tm), lambda n, j: (0, n)),
        out_shape=jax.ShapeDtypeStruct((d, s), F32),
        scratch_shapes=[pltpu.VMEM((half, tm), F32), pltpu.VMEM((half, tm), F32),
                        pltpu.VMEM((PEER_HEADS, N_KEYS, tm), F32)],
        compiler_params=_cparams(("parallel", "arbitrary")),
        name="peer_dense",
    )(xn, down, up_t, up_t, s1, s2, rows)


def _add_t_kernel(a_ref, bt_ref, o_ref):
    for c in range(a_ref.shape[1] // LANES):
        sl = slice(c * LANES, (c + 1) * LANES)
        o_ref[:, sl] = a_ref[:, sl] + bt_ref[sl, :].T


def _add_transposed(a, b_t, tm=256):
    s, d = a.shape
    spec = pl.BlockSpec((tm, d), lambda i: (i, 0))
    return pl.pallas_call(
        _add_t_kernel, grid=(s // tm,),
        in_specs=[spec, pl.BlockSpec((d, tm), lambda i: (0, i))], out_specs=spec,
        out_shape=jax.ShapeDtypeStruct((s, d), F32),
        compiler_params=_cparams(("parallel",)), name="residual_add",
    )(a, b_t)


def _tile(n, pref):
    t = pref
    while n % t:
        t //= 2
    return t


def _layer(x, attn_norm_g, w_in, qk_gain_a, qk_gain_b, rel_bias, cmp_pos, cmp_w1, cmp_w2,
           out_norm_g, w_out, ffn_norm_g, peer_w_query, peer_sub_keys, peer_down, peer_up):
    s, d = x.shape
    scale = HEAD_DIM ** -0.5
    ones = jnp.ones((HEAD_DIM,), F32)
    zeros = jnp.zeros((HEAD_DIM,), F32)

    xn = _rmsnorm(x, attn_norm_g)
    q_scale = scale * LOG2E
    gain_a = jnp.concatenate([qk_gain_a[0] * q_scale] * 16 + [qk_gain_a[1]] * 16 + [ones] * 16)
    mode_a = jnp.concatenate([ones] * 32 + [zeros] * 16)
    gain_b = jnp.concatenate([qk_gain_b[0] * q_scale] * 16 + [ones] * 8 + [qk_gain_b[2]] * 4 + [ones] * 4
                             + [qk_gain_b[3]] * 4 + [ones] * 4)
    mode_b = jnp.concatenate([ones] * 16 + [zeros] * 8 + [ones] * 4 + [zeros] * 4 + [ones] * 4 + [zeros] * 4)
    n_gate = w_in.shape[1] - PROJ_COLS
    w_gate = jnp.pad(w_in[:, PROJ_COLS:], ((0, 0), (0, LANES - n_gate))).astype(BF16)
    tm = _tile(s, 1024)
    dilations = tuple(dil for _, dil in DILATED_PATTERNS)
    proj_a = _proj_headnorm(xn, w_in[:, :PROJ_A_COLS].astype(BF16), gain_a.astype(F32).reshape(1, -1),
                            mode_a.reshape(1, -1), tm, 512, dilations)
    proj = _proj_headnorm(xn, w_in[:, PROJ_A_COLS:PROJ_COLS].astype(BF16), gain_b.astype(F32).reshape(1, -1),
                          mode_b.reshape(1, -1), tm, 1024)[0]
    gl = _matmul(xn, w_gate, F32, tm, LANES)

    table_a = rel_bias[:N_HEADS_A] * LOG2E
    table_b = rel_bias[N_HEADS_A:] * LOG2E

    o_dil, lse_dil = [], []
    hb = 8
    for (window, dil), arr in zip(DILATED_PATTERNS, proj_a):
        arr = arr.reshape(dil, s // dil, PROJ_A_COLS)
        bias = _band_bias(table_a, window // dil, dil, 1).reshape(N_HEADS_A // hb, hb, BAND_TQ, 2 * BAND_TQ)
        o, lse = _banded_attention(
            arr, bias, n_hb=N_HEADS_A // hb, hb=hb, r=1, nprev=1,
            qcol=lambda hh: COL_QA // hb + hh, kcol=lambda hh: COL_KA // hb + hh, vcol=lambda hh: COL_VA // hb + hh,
            with_lse=True)
        o_dil.append(o[0] if dil == 1 else o)
        lse_dil.append(lse[0] if dil == 1 else lse)

    nprev_w = -(-(WIN_B - 1) // BAND_TQ)
    gw = 2
    bias_w = _band_bias(table_b, WIN_B - 1, 1, nprev_w).reshape(
        N_KV_B // gw, gw * Q_PER_KV_B, BAND_TQ, (nprev_w + 1) * BAND_TQ)
    o_win = _banded_attention(
        proj[None], bias_w, n_hb=N_KV_B // gw, hb=gw * Q_PER_KV_B, r=Q_PER_KV_B, nprev=nprev_w,
        qcol=lambda hh: COL_QB // (gw * Q_PER_KV_B) + hh, kcol=lambda hh: COL_KWIN // gw + hh,
        vcol=lambda hh: COL_VWIN // gw + hh, with_lse=False)[0][0]

    n_chunk = s // CMP_STRIDE
    n_cmp = (s - CMP_BLOCK) // CMP_STRIDE + 1
    raw = proj[:, COL_KCMP * HEAD_DIM:(COL_VCMP + N_KV_B) * HEAD_DIM]
    xt = raw.reshape(n_chunk, CMP_STRIDE, 2, N_KV_B, HEAD_DIM).transpose(2, 3, 1, 0, 4)
    kv_c = _compress(xt, cmp_w1.astype(BF16), cmp_w2.astype(BF16),
                     cmp_pos.reshape(2, 1, CMP_BLOCK * HEAD_DIM).astype(BF16),
                     qk_gain_b[1].reshape(1, HEAD_DIM).astype(F32))
    n_rows = CMP_PAD + n_chunk + 16
    kv_c = jnp.pad(kv_c[:, :, :n_cmp], ((0, 0), (0, 0), (CMP_PAD, n_rows - CMP_PAD - n_cmp), (0, 0)))
    kcp = kv_c[0]
    vcpt = kv_c[1].transpose(0, 2, 1)

    tl = np.arange(CMP_TQ)[None, :]
    il = np.arange(CMP_BAND)[:, None]
    rel_near = CMP_STRIDE * CMP_PAD - (CMP_BLOCK - 1) + tl - CMP_STRIDE * il
    bnear = _bias_of_rel(table_b, rel_near).reshape(N_KV_B, Q_PER_KV_B, CMP_BAND, CMP_TQ)
    bnear = bnear.transpose(0, 2, 1, 3).reshape(N_KV_B, CMP_BAND, Q_PER_KV_B * CMP_TQ)
    b31 = table_b[:, N_BUCKETS - 1].astype(F32)
    far_hi = b31.astype(BF16).astype(F32)
    far_lo = (b31 - far_hi).astype(BF16).astype(F32)
    bfar = jnp.stack([jnp.repeat(v.reshape(N_KV_B, Q_PER_KV_B), CMP_TQ, axis=1) for v in (far_hi, far_lo)], axis=1)
    bfar = jnp.pad(bfar, ((0, 0), (0, 6), (0, 0)))
    caug_np = np.zeros((n_rows, CMP_AUG), np.float32)
    caug_np[:, 0:2] = 1.0
    caug_np[:CMP_PAD, 2] = 1.0
    caug_np[np.arange(n_rows), 3 + np.arange(n_rows) // 16] = 1.0
    kc_aug = jnp.concatenate([kcp, jnp.broadcast_to(jnp.asarray(caug_np, BF16)[None], (N_KV_B, n_rows, CMP_AUG))],
                             axis=2)
    n_slc = s // SLC_BLOCK
    ratio = SLC_BLOCK // CMP_STRIDE
    tt_np = np.zeros((n_slc, n_rows), np.float32)
    for jblk in range(n_slc):
        for off, wgt in ((-1, 1.0), (0, 2.0), (1, 2.0), (2, 2.0), (3, 1.0)):
            i_c = ratio * jblk + off
            if 0 <= i_c < n_cmp:
                tt_np[jblk, CMP_PAD + i_c] = wgt
    tt = jnp.asarray(tt_np, BF16)
    o_cmp, mask = _cmp_select(proj, kc_aug, vcpt, bnear, bfar, tt)

    vt = proj[:, COL_VSLC * HEAD_DIM:(COL_VSLC + N_KV_B) * HEAD_DIM].reshape(s, N_KV_B, HEAD_DIM).transpose(1, 2, 0)
    vt = jnp.concatenate([vt, jnp.ones((N_KV_B, 16, s), BF16)], axis=1)
    kl = np.arange(CMP_TQ)[:, None]
    od = np.arange(SLC_NEAR)[:, None, None]
    rel_s = CMP_TQ * od + tl[None] - kl[None]
    btile = _bias_of_rel(table_b, rel_s).reshape(N_KV_B, Q_PER_KV_B, SLC_NEAR, CMP_TQ, CMP_TQ)
    bdiff = btile - (far_hi + far_lo).reshape(N_KV_B, Q_PER_KV_B, 1, 1, 1)
    bdiff = bdiff.transpose(0, 2, 3, 1, 4).reshape(N_KV_B, SLC_NEAR, CMP_TQ, Q_PER_KV_B * CMP_TQ)
    aug_np = np.zeros((s, SLC_AUG), np.float32)
    tk = CMP_TQ
    aug_np[np.arange(s), (np.arange(s) % tk) // SLC_BLOCK] = 1.0
    aug_np[:, tk // SLC_BLOCK:tk // SLC_BLOCK + 2] = 1.0
    k_slc = proj[:, COL_KSLC * HEAD_DIM:(COL_KSLC + N_KV_B) * HEAD_DIM].reshape(s, N_KV_B, HEAD_DIM)
    k_aug = jnp.concatenate([k_slc.transpose(1, 0, 2),
                             jnp.broadcast_to(jnp.asarray(aug_np, BF16)[None], (N_KV_B, s, SLC_AUG))], axis=2)
    o_slc = _slc_attention(proj, k_aug, vt, mask, bdiff, bfar)

    o_n = _combine(o_dil, lse_dil, o_cmp, o_slc, o_win, gl, out_norm_g.astype(F32))
    h = _matmul_residual(o_n, w_out.astype(BF16), x, tm, _tile(d, 1024))

    xn2 = _rmsnorm(h, ffn_norm_g)
    qp = _matmul(xn2, peer_w_query.astype(BF16), BF16, tm, 1024)
    s1, s2, rows = _peer_topk(qp, peer_sub_keys.astype(BF16))
    peer_t = _peer_dense(xn2, peer_down.astype(BF16), peer_up.T.astype(BF16), s1, s2, rows, _tile(s, 512), 512)
    return _add_transposed(h, peer_t)


def kernel(x, attn_norm_g, w_in, qk_gain_a, qk_gain_b, rel_bias, cmp_pos, cmp_w1, cmp_w2, out_norm_g, w_out,
           ffn_norm_g, peer_w_query, peer_sub_keys, peer_down, peer_up):
    b = x.shape[0]
    outs = [_layer(x[bi], attn_norm_g[0], w_in[0], qk_gain_a[0], qk_gain_b[0], rel_bias, cmp_pos[0], cmp_w1[0],
                   cmp_w2[0], out_norm_g[0], w_out[0], ffn_norm_g[0], peer_w_query[0], peer_sub_keys[0],
                   peer_down[0], peer_up[0]) for bi in range(b)]
    return jnp.stack(outs, axis=0)
```

```python
import functools
import math

import jax
import jax.numpy as jnp
import numpy as np
from jax import lax
from jax.experimental import pallas as pl
from jax.experimental.pallas import tpu as pltpu

F32 = jnp.float32
BF16 = jnp.bfloat16

HEAD_DIM = 128
LANES = 128
N_HEADS_A = 16
N_HEADS_B = 16
N_KV_B = 4
Q_PER_KV_B = 4
DILATED_PATTERNS = ((128, 1), (512, 4), (2048, 16))
CMP_BLOCK = 32
CMP_STRIDE = 16
SLC_BLOCK = 64
N_SLC = 16
WIN_B = 512
N_BUCKETS = 32
MAX_DISTANCE = 2048
PEER_HEADS = 8
N_KEYS = 128
PEER_TOPK = 16
RMS_EPS = 1e-6
NEG = -1e30
BIG = 3e38
LOG2E = math.log2(math.e)

COL_QA, COL_KA, COL_VA = 0, 16, 32
N_PROJ_A_HEADS = 48
COL_QB, COL_KCMP, COL_VCMP, COL_KSLC, COL_VSLC, COL_KWIN, COL_VWIN = 0, 16, 20, 24, 28, 32, 36
N_PROJ_B_HEADS = 40
PROJ_A_COLS = N_PROJ_A_HEADS * HEAD_DIM
PROJ_COLS = (N_PROJ_A_HEADS + N_PROJ_B_HEADS) * HEAD_DIM

BAND_TQ = 128
CMP_TQ = 256
CMP_PAD = 112
CMP_BAND = 128
SLC_NEAR = 8
SLC_QT = 2
SLC_AUG = 16
CMP_AUG = 128
VMEM_LIMIT = 56 * 1024 * 1024


def _cparams(sem):
    return pltpu.CompilerParams(dimension_semantics=sem, vmem_limit_bytes=VMEM_LIMIT)


def _rel_bucket(dist):
    n = np.maximum(dist, 0)
    max_exact = N_BUCKETS // 2
    nf = np.maximum(n, 1).astype(np.float32)
    log_part = (np.log(nf / np.float32(max_exact)) / np.float32(math.log(MAX_DISTANCE / max_exact))
                * np.float32(N_BUCKETS - max_exact))
    large = np.minimum(max_exact + log_part.astype(np.int32), N_BUCKETS - 1)
    return np.where(n < max_exact, n, large).astype(np.int32)


def _bias_of_rel(table, rel):
    rel = np.asarray(rel)
    bucket = np.where(rel >= 0, _rel_bucket(rel), N_BUCKETS).reshape(-1)
    onehot = (jnp.asarray(bucket)[None, :] == jnp.arange(N_BUCKETS + 1)[:, None]).astype(F32)
    table = jnp.concatenate([table.astype(F32), jnp.full((table.shape[0], 1), NEG, F32)], axis=1)
    out = jnp.dot(table, onehot, precision=lax.Precision.HIGHEST)
    return out.reshape((table.shape[0],) + rel.shape)


def _rmsnorm_kernel(x_ref, g_ref, o_ref):
    x = x_ref[...]
    ms = jnp.mean(x * x, axis=-1, keepdims=True)
    o_ref[...] = (x * lax.rsqrt(ms + RMS_EPS) * g_ref[...]).astype(o_ref.dtype)


def _rmsnorm(x, g, tm=256):
    s, d = x.shape
    return pl.pallas_call(
        _rmsnorm_kernel,
        grid=(s // tm,),
        in_specs=[pl.BlockSpec((tm, d), lambda i: (i, 0)), pl.BlockSpec((1, d), lambda i: (0, 0))],
        out_specs=pl.BlockSpec((tm, d), lambda i: (i, 0)),
        out_shape=jax.ShapeDtypeStruct((s, d), BF16),
        compiler_params=_cparams(("parallel",)),
        name="rmsnorm",
    )(x, g.reshape(1, d).astype(F32))


def _proj_kernel(x_ref, w_ref, gain_ref, mode_ref, *refs, n_chunks, dilations, tm):
    outs = refs[:len(dilations)]
    acc = jnp.dot(x_ref[...], w_ref[...], preferred_element_type=F32)
    for c in range(n_chunks):
        sl = slice(c * LANES, (c + 1) * LANES)
        t = acc[:, sl]
        ms = jnp.mean(t * t, axis=-1, keepdims=True)
        mult = jnp.where(mode_ref[:, sl] > 0, lax.rsqrt(ms + RMS_EPS) * gain_ref[:, sl], 1.0)
        y = t * mult
        if len(dilations) > 1:
            y_scr = refs[len(dilations)]
            y_scr[c] = y
        for o_ref, dil in zip(outs, dilations):
            if dil == 1:
                o_ref[:, sl] = y.astype(o_ref.dtype)
            else:
                for r in range(dil):
                    o_ref[r, :, sl] = y_scr[c, pl.ds(r, tm // dil, stride=dil), :].astype(o_ref.dtype)


def _proj_headnorm(xn, w, gain, mode, tm, tn, dilations=(1,)):
    s, d = xn.shape
    n = w.shape[1]
    out_specs, out_shape = [], []
    for dil in dilations:
        if dil == 1:
            out_specs.append(pl.BlockSpec((tm, tn), lambda i, j: (i, j)))
            out_shape.append(jax.ShapeDtypeStruct((s, n), BF16))
        else:
            out_specs.append(pl.BlockSpec((dil, tm // dil, tn), lambda i, j: (0, i, j)))
            out_shape.append(jax.ShapeDtypeStruct((dil, s // dil, n), BF16))
    scratch = [pltpu.VMEM((tn // LANES, tm, LANES), F32)] if len(dilations) > 1 else []
    return pl.pallas_call(
        functools.partial(_proj_kernel, n_chunks=tn // LANES, dilations=tuple(dilations), tm=tm),
        grid=(s // tm, n // tn),
        in_specs=[pl.BlockSpec((tm, d), lambda i, j: (i, 0)),
                  pl.BlockSpec((d, tn), lambda i, j: (0, j)),
                  pl.BlockSpec((1, tn), lambda i, j: (0, j)),
                  pl.BlockSpec((1, tn), lambda i, j: (0, j))],
        out_specs=out_specs,
        out_shape=out_shape,
        scratch_shapes=scratch,
        compiler_params=_cparams(("parallel", "arbitrary")),
        name="proj_headnorm",
    )(xn, w, gain, mode)


def _matmul_kernel(x_ref, w_ref, o_ref):
    o_ref[...] = jnp.dot(x_ref[...], w_ref[...], preferred_element_type=F32).astype(o_ref.dtype)


def _matmul(x, w, out_dtype, tm, tn):
    s, d = x.shape
    n = w.shape[1]
    return pl.pallas_call(
        _matmul_kernel,
        grid=(s // tm, n // tn),
        in_specs=[pl.BlockSpec((tm, d), lambda i, j: (i, 0)), pl.BlockSpec((d, tn), lambda i, j: (0, j))],
        out_specs=pl.BlockSpec((tm, tn), lambda i, j: (i, j)),
        out_shape=jax.ShapeDtypeStruct((s, n), out_dtype),
        compiler_params=_cparams(("parallel", "arbitrary")),
        name="matmul",
    )(x, w)


def _matmul_res_kernel(x_ref, w_ref, r_ref, o_ref):
    o_ref[...] = r_ref[...] + jnp.dot(x_ref[...], w_ref[...], preferred_element_type=F32)


def _matmul_residual(x, w, res, tm, tn):
    s, d = x.shape
    n = w.shape[1]
    return pl.pallas_call(
        _matmul_res_kernel,
        grid=(s // tm, n // tn),
        in_specs=[pl.BlockSpec((tm, d), lambda i, j: (i, 0)),
                  pl.BlockSpec((d, tn), lambda i, j: (0, j)),
                  pl.BlockSpec((tm, tn), lambda i, j: (i, j))],
        out_specs=pl.BlockSpec((tm, tn), lambda i, j: (i, j)),
        out_shape=jax.ShapeDtypeStruct((s, n), F32),
        compiler_params=_cparams(("parallel", "arbitrary")),
        name="matmul_residual",
    )(x, w, res)


def _banded_kernel(*refs, hb, r, nprev, tq, with_lse):
    nk = nprev + 1
    q_ref = refs[0]
    k_refs = refs[1:1 + nk]
    v_refs = refs[1 + nk:1 + 2 * nk]
    bias_ref = refs[1 + 2 * nk]
    o_ref = refs[2 + 2 * nk]
    i = pl.program_id(2)
    span = nk * tq
    ng = hb // r
    q3 = jnp.stack([jnp.concatenate([q_ref[:, (g * r + rr) * LANES:(g * r + rr + 1) * LANES] for rr in range(r)],
                                    axis=0) for g in range(ng)], axis=0)
    parts = []
    for j in range(nk):
        k3 = jnp.stack([k_refs[j][:, g * LANES:(g + 1) * LANES] for g in range(ng)], axis=0)
        sj = jnp.einsum('gqd,gkd->gqk', q3, k3, preferred_element_type=F32)
        if j < nprev:
            sj = sj + jnp.where(i >= nprev - j, 0.0, NEG)
        parts.append(sj)
    s = jnp.concatenate(parts, axis=2) + bias_ref[...].reshape(ng, r * tq, span)
    m = jnp.max(s, axis=-1, keepdims=True)
    p = jnp.exp2(s - m)
    den = jnp.sum(p, axis=-1, keepdims=True)
    pb = p.astype(BF16)
    o = None
    for j in range(nk):
        v3 = jnp.stack([v_refs[j][:, g * LANES:(g + 1) * LANES] for g in range(ng)], axis=0)
        oj = jnp.einsum('gqk,gkd->gqd', pb[:, :, j * tq:(j + 1) * tq], v3, preferred_element_type=F32)
        o = oj if o is None else o + oj
    o = o / den
    for g in range(ng):
        for rr in range(r):
            h = g * r + rr
            o_ref[:, h * LANES:(h + 1) * LANES] = o[g, rr * tq:(rr + 1) * tq].astype(o_ref.dtype)
    if with_lse:
        lse_ref = refs[3 + 2 * nk]
        lse = m + jnp.log(den) * LOG2E
        lane = lax.broadcasted_iota(jnp.int32, (tq, LANES), 1)
        lse_mat = jnp.zeros((tq, LANES), F32)
        for g in range(ng):
            lse_mat = jnp.where(lane == g, lse[g], lse_mat)
        lse_ref[...] = lse_mat


def _banded_attention(arr, bias, *, n_hb, hb, r, nprev, qcol, kcol, vcol, with_lse):
    tq = BAND_TQ
    n_r, length = arr.shape[0], arr.shape[1]
    nk = nprev + 1
    qw = hb * LANES
    kw = (hb // r) * LANES

    def kmap(col, back):
        return lambda rr, hh, i: (rr, jnp.maximum(i - back, 0), col(hh))

    in_specs = [pl.BlockSpec((None, tq, qw), lambda rr, hh, i: (rr, i, qcol(hh)))]
    in_specs += [pl.BlockSpec((None, tq, kw), kmap(kcol, nprev - j)) for j in range(nk)]
    in_specs += [pl.BlockSpec((None, tq, kw), kmap(vcol, nprev - j)) for j in range(nk)]
    in_specs += [pl.BlockSpec((None, hb, tq, nk * tq), lambda rr, hh, i: (hh, 0, 0, 0))]
    out_specs = [pl.BlockSpec((None, tq, qw), lambda rr, hh, i: (rr, i, hh))]
    out_shape = [jax.ShapeDtypeStruct((n_r, length, n_hb * qw), BF16)]
    if with_lse:
        out_specs.append(pl.BlockSpec((None, tq, LANES), lambda rr, hh, i: (rr, i, hh)))
        out_shape.append(jax.ShapeDtypeStruct((n_r, length, n_hb * LANES), F32))
    return pl.pallas_call(
        functools.partial(_banded_kernel, hb=hb, r=r, nprev=nprev, tq=tq, with_lse=with_lse),
        grid=(n_r, n_hb, length // tq),
        in_specs=in_specs,
        out_specs=out_specs,
        out_shape=out_shape,
        compiler_params=_cparams(("parallel", "parallel", "arbitrary")),
        name="banded_attention",
    )(*([arr] * (1 + 2 * nk)), bias)


def _band_bias(table, max_dist, dist_scale, nprev):
    tq = BAND_TQ
    q_loc = np.arange(tq)[:, None]
    k_loc = np.arange((nprev + 1) * tq)[None, :] - nprev * tq
    rel = q_loc - k_loc
    return _bias_of_rel(table, np.where((rel >= 0) & (rel <= max_dist), rel * dist_scale, -1))


def _compress_kernel(x_ref, w1_ref, w2_ref, pos_ref, gain_ref, o_ref, *, n_chunk):
    half = CMP_STRIDE
    a = jnp.zeros((n_chunk, HEAD_DIM), F32)
    b = jnp.zeros((n_chunk, HEAD_DIM), F32)
    for c in range(half):
        xc = x_ref[c]
        a = a + jnp.dot(xc, w1_ref[c * HEAD_DIM:(c + 1) * HEAD_DIM, :], preferred_element_type=F32)
        b = b + jnp.dot(xc, w1_ref[(half + c) * HEAD_DIM:(half + c + 1) * HEAD_DIM, :],
                        preferred_element_type=F32)
    pos = jnp.broadcast_to(pos_ref[...], (8, CMP_BLOCK * HEAD_DIM))
    posterm = jnp.dot(pos, w1_ref[...], preferred_element_type=F32)[0:1, :]
    pre = a + pltpu.roll(b, n_chunk - 1, 0) + posterm
    hid = jax.nn.gelu(pre)
    out = jnp.dot(hid.astype(BF16), w2_ref[...], preferred_element_type=F32)
    ms = jnp.mean(out * out, axis=-1, keepdims=True)
    normed = out * lax.rsqrt(ms + RMS_EPS) * gain_ref[...]
    o_ref[...] = jnp.where(pl.program_id(0) == 0, normed, out).astype(o_ref.dtype)


def _compress(xt, w1, w2, pos, gain):
    n_chunk = xt.shape[3]
    return pl.pallas_call(
        functools.partial(_compress_kernel, n_chunk=n_chunk),
        grid=(2, N_KV_B),
        in_specs=[pl.BlockSpec((None, None, CMP_STRIDE, n_chunk, HEAD_DIM), lambda w, g: (w, g, 0, 0, 0)),
                  pl.BlockSpec((None, CMP_BLOCK * HEAD_DIM, HEAD_DIM), lambda w, g: (w, 0, 0)),
                  pl.BlockSpec((None, HEAD_DIM, HEAD_DIM), lambda w, g: (w, 0, 0)),
                  pl.BlockSpec((None, 1, CMP_BLOCK * HEAD_DIM), lambda w, g: (w, 0, 0)),
                  pl.BlockSpec((1, HEAD_DIM), lambda w, g: (0, 0))],
        out_specs=pl.BlockSpec((None, None, n_chunk, HEAD_DIM), lambda w, g: (w, g, 0, 0)),
        out_shape=jax.ShapeDtypeStruct((2, N_KV_B, n_chunk, HEAD_DIM), BF16),
        compiler_params=_cparams(("parallel", "parallel")),
        name="nsa_compress",
    )(xt, w1, w2, pos, gain)


def _cmp_select_kernel(q_ref, kc_ref, vct_ref, bnear_ref, bfar_ref, tt_ref, o_ref, mask_ref, p_scr, imp_scr, *, tq):
    n = pl.program_id(1)
    t0 = n * tq
    wide = Q_PER_KV_B * tq
    band_lo = pl.multiple_of(n * (tq // CMP_STRIDE), 16)
    t = t0 + lax.broadcasted_iota(jnp.int32, (1, tq), 1)
    valid = jnp.concatenate([t >= CMP_BLOCK - 1] * Q_PER_KV_B, axis=1)
    q_t = jnp.concatenate([q_ref[:, r * LANES:(r + 1) * LANES].astype(F32).T.astype(BF16)
                           for r in range(Q_PER_KV_B)], axis=1)
    ri = lax.broadcasted_iota(jnp.int32, (CMP_AUG, wide), 0)
    blocked = (ri == 2) | ((ri >= 3) & (ri - 3 >= n * (tq // CMP_STRIDE) // 16))
    aug = jnp.where(ri == 0, bfar_ref[0:1, :], jnp.where(ri == 1, bfar_ref[1:2, :],
                                                         jnp.where(blocked, NEG, 0.0)))
    rhs = jnp.concatenate([q_t, aug.astype(BF16)], axis=0)
    brow = lax.broadcasted_iota(jnp.int32, (CMP_BAND, wide), 0) + band_lo
    band_pen = jnp.where(brow >= CMP_PAD, 0.0, NEG)

    def attend(nr):
        s_far = jnp.dot(kc_ref[0:nr, :], rhs, preferred_element_type=F32)
        s_band = jnp.dot(kc_ref[pl.ds(band_lo, CMP_BAND), 0:HEAD_DIM], q_t, preferred_element_type=F32)
        s_band = s_band + bnear_ref[...] + band_pen
        m = jnp.maximum(jnp.max(s_far, axis=0, keepdims=True), jnp.max(s_band, axis=0, keepdims=True))
        p_far = jnp.exp2(s_far - m)
        p_band = jnp.exp2(s_band - m)
        den = jnp.sum(p_far, axis=0, keepdims=True) + jnp.sum(p_band, axis=0, keepdims=True)
        inv = jnp.where(valid, 1.0 / den, 0.0)
        p_scr[0:nr, :] = p_far * inv
        p_scr[pl.ds(band_lo, CMP_BAND), :] = p_band * inv
        p = p_scr[0:nr, :]
        o_t = jnp.dot(vct_ref[:, 0:nr], p.astype(BF16), preferred_element_type=F32)
        psum = p[:, 0:tq]
        for r in range(Q_PER_KV_B):
            o_ref[:, r * LANES:(r + 1) * LANES] = o_t[:, r * tq:(r + 1) * tq].T.astype(o_ref.dtype)
            if r:
                psum = psum + p[:, r * tq:(r + 1) * tq]
        p_hi = psum.astype(BF16)
        p_lo = (psum - p_hi.astype(F32)).astype(BF16)
        imp_scr[...] = (jnp.dot(tt_ref[:, 0:nr], p_hi, preferred_element_type=F32)
                        + jnp.dot(tt_ref[:, 0:nr], p_lo, preferred_element_type=F32))

    n_rows = kc_ref.shape[0]
    step = -(-(n_rows - CMP_BAND) // (4 * LANES)) * LANES
    caps = sorted({min(n_rows, CMP_BAND + k * step) for k in range(1, 5)})
    band_top = band_lo + CMP_BAND
    prev_cap = 0
    for cap in caps:
        pl.when((band_top > prev_cap) & (band_top <= cap))(functools.partial(attend, cap))
        prev_cap = cap
    imp = imp_scr[...]
    n_slc = imp.shape[0]
    jj = lax.broadcasted_iota(jnp.int32, (n_slc, tq), 0)
    cur = t // SLC_BLOCK
    allowed = jj <= cur
    score = jnp.where(jj == 0, 3e9,
                      jnp.where(jj == cur, 2e9,
                                jnp.where(jj == cur - 1, 1e9, jnp.where(allowed, imp, NEG))))
    work = score
    thr = jnp.zeros((1, tq), F32)
    for _ in range(N_SLC):
        thr = jnp.max(work, axis=0, keepdims=True)
        work = jnp.where(work >= thr, -BIG, work)
    mask_ref[...] = jnp.where((score >= thr) & allowed, 0.0, NEG)


def _cmp_select(proj, kc_aug, vcpt, bnear, bfar, tt):
    s = proj.shape[0]
    tq = CMP_TQ
    wide = Q_PER_KV_B * tq
    n_rows = kc_aug.shape[1]
    n_slc = tt.shape[0]
    return pl.pallas_call(
        functools.partial(_cmp_select_kernel, tq=tq),
        grid=(N_KV_B, s // tq),
        in_specs=[pl.BlockSpec((tq, Q_PER_KV_B * LANES), lambda g, n: (n, COL_QB // Q_PER_KV_B + g)),
                  pl.BlockSpec((None, n_rows, HEAD_DIM + CMP_AUG), lambda g, n: (g, 0, 0)),
                  pl.BlockSpec((None, HEAD_DIM, n_rows), lambda g, n: (g, 0, 0)),
                  pl.BlockSpec((None, CMP_BAND, wide), lambda g, n: (g, 0, 0)),
                  pl.BlockSpec((None, 8, wide), lambda g, n: (g, 0, 0)),
                  pl.BlockSpec((n_slc, n_rows), lambda g, n: (0, 0))],
        out_specs=[pl.BlockSpec((tq, Q_PER_KV_B * LANES), lambda g, n: (n, g)),
                   pl.BlockSpec((None, n_slc, tq), lambda g, n: (g, 0, n))],
        out_shape=[jax.ShapeDtypeStruct((s, N_HEADS_B * HEAD_DIM), BF16),
                   jax.ShapeDtypeStruct((N_KV_B, n_slc, s), F32)],
        scratch_shapes=[pltpu.VMEM((n_rows, wide), F32), pltpu.VMEM((n_slc, tq), F32)],
        compiler_params=_cparams(("parallel", "arbitrary")),
        name="nsa_cmp_select",
    )(proj, kc_aug, vcpt, bnear, bfar, tt)


def _slc_kernel(q_ref, k_ref, vt_ref, mask_ref, bdiff_ref, bfar_ref, o_ref,
                s_scr, p_scr, alpha_scr, m_scr, acc_scr, *, tq):
    qt = SLC_QT
    first = pl.program_id(1) * qt
    last = first + qt - 1
    tk = tq
    blocks_per_chunk = tk // SLC_BLOCK
    tile_w = Q_PER_KV_B * tq
    wide = qt * tile_w
    q_t = jnp.concatenate([q_ref[t * tq:(t + 1) * tq, r * LANES:(r + 1) * LANES].astype(F32).T.astype(BF16)
                           for t in range(qt) for r in range(Q_PER_KV_B)], axis=1)
    tail_rows = [jnp.concatenate([bfar_ref[0:1, :]] * qt, axis=1), jnp.concatenate([bfar_ref[1:2, :]] * qt, axis=1),
                 jnp.zeros((SLC_AUG - blocks_per_chunk - 2, wide), F32)]

    def scores(c, slot, near):
        pens = [jnp.where(c > first + t, NEG, 0.0) for t in range(qt)]
        c = jnp.minimum(c, last)
        k0 = pl.multiple_of(c * tk, tk)
        mrows = []
        for b in range(blocks_per_chunk):
            row = mask_ref[pl.ds(c * blocks_per_chunk + b, 1), :]
            mrows.append(jnp.concatenate([row[:, t * tq:(t + 1) * tq] + pens[t]
                                          for t in range(qt) for _ in range(Q_PER_KV_B)], axis=1))
        aug = jnp.concatenate(mrows + tail_rows, axis=0).astype(BF16)
        rhs = jnp.concatenate([q_t, aug], axis=0)
        s = jnp.dot(k_ref[pl.ds(k0, tk), :], rhs, preferred_element_type=F32)
        if near:
            s = jnp.concatenate(
                [s[:, t * tile_w:(t + 1) * tile_w] + bdiff_ref[jnp.clip(first + t - c, 0, SLC_NEAR - 1)]
                 for t in range(qt)], axis=1)
        s_scr[slot] = s

    def softmax_update(slot):
        s = s_scr[slot]
        m_old = m_scr[...]
        m_new = jnp.maximum(m_old, jnp.max(s, axis=0, keepdims=True))
        alpha = jnp.exp2(m_old - m_new)
        p = jnp.exp2(s - m_new)
        m_scr[...] = m_new
        alpha_scr[slot] = alpha
        p_scr[slot] = p.astype(BF16)

    def values(c, slot):
        k0 = pl.multiple_of(jnp.clip(c, 0, last) * tk, tk)
        acc_scr[...] = alpha_scr[slot] * acc_scr[...] + jnp.dot(
            vt_ref[:, pl.ds(k0, tk)], p_scr[slot], preferred_element_type=F32)

    m_scr[...] = jnp.full(m_scr.shape, -BIG, F32)
    acc_scr[...] = jnp.zeros(acc_scr.shape, F32)
    p_scr[1] = jnp.zeros(p_scr.shape[1:], BF16)
    alpha_scr[1] = jnp.ones(alpha_scr.shape[1:], F32)
    scores(0, 0, True)

    def body(i, near):
        c = 2 * i
        scores(c + 1, 1, near)
        softmax_update(0)
        values(c - 1, 1)
        scores(c + 2, 0, near)
        softmax_update(1)
        values(c, 0)

    def far_body(i, carry):
        body(i, False)
        return carry

    def near_body(i, carry):
        body(i, True)
        return carry

    n_pairs = (last + 2) // 2
    n_far = jnp.clip((first - (SLC_NEAR - 1)) // 2, 0, n_pairs)
    lax.fori_loop(0, n_far, far_body, 0)
    lax.fori_loop(n_far, n_pairs, near_body, 0)
    values(2 * n_pairs - 1, 1)
    o_t = acc_scr[0:HEAD_DIM, :] / acc_scr[HEAD_DIM:HEAD_DIM + 1, :]
    for t in range(qt):
        for r in range(Q_PER_KV_B):
            lanes = slice((t * Q_PER_KV_B + r) * tq, (t * Q_PER_KV_B + r + 1) * tq)
            o_ref[t * tq:(t + 1) * tq, r * LANES:(r + 1) * LANES] = o_t[:, lanes].T.astype(o_ref.dtype)


def _slc_attention(proj, k_aug, vt, mask, bdiff, bfar):
    s = proj.shape[0]
    tq = CMP_TQ
    rows = SLC_QT * tq
    n_slc = mask.shape[1]
    tile_w = Q_PER_KV_B * tq
    wide = SLC_QT * tile_w
    v_rows = vt.shape[1]
    return pl.pallas_call(
        functools.partial(_slc_kernel, tq=tq),
        grid=(N_KV_B, s // rows),
        in_specs=[pl.BlockSpec((rows, Q_PER_KV_B * LANES), lambda g, n: (n, COL_QB // Q_PER_KV_B + g)),
                  pl.BlockSpec((None, s, HEAD_DIM + SLC_AUG), lambda g, n: (g, 0, 0)),
                  pl.BlockSpec((None, v_rows, s), lambda g, n: (g, 0, 0)),
                  pl.BlockSpec((None, n_slc, rows), lambda g, n: (g, 0, n)),
                  pl.BlockSpec((None, SLC_NEAR, tq, tile_w), lambda g, n: (g, 0, 0, 0)),
                  pl.BlockSpec((None, 8, tile_w), lambda g, n: (g, 0, 0))],
        out_specs=pl.BlockSpec((rows, Q_PER_KV_B * LANES), lambda g, n: (n, g)),
        out_shape=jax.ShapeDtypeStruct((s, N_HEADS_B * HEAD_DIM), BF16),
        scratch_shapes=[pltpu.VMEM((2, tq, wide), F32),
                        pltpu.VMEM((2, tq, wide), BF16),
                        pltpu.VMEM((2, 1, wide), F32),
                        pltpu.VMEM((1, wide), F32),
                        pltpu.VMEM((v_rows, wide), F32)],
        compiler_params=_cparams(("parallel", "arbitrary")),
        name="nsa_selected",
    )(proj, k_aug, vt, mask, bdiff, bfar)


def _combine_kernel(o1_ref, o2_ref, o3_ref, l1_ref, l2_ref, l3_ref, oc_ref, os_ref, ow_ref, gl_ref, g_ref, o_ref,
                    o_scr, l_scr, *, tm):
    heads_per_lse_block = 8
    n_lse_blocks = N_HEADS_A // heads_per_lse_block
    dils = [dil for _, dil in DILATED_PATTERNS]
    o_in = [o1_ref, o2_ref, o3_ref]
    l_in = [l1_ref, l2_ref, l3_ref]
    for pi, dil in enumerate(dils):
        if dil == 1:
            continue
        rows = tm // dil
        for r in range(dil):
            for h in range(N_HEADS_A):
                o_scr[pi, h, pl.ds(r, rows, stride=dil), :] = o_in[pi][r, :, h * LANES:(h + 1) * LANES].astype(F32)
            for b in range(n_lse_blocks):
                l_scr[pi, b, pl.ds(r, rows, stride=dil), :] = l_in[pi][r, :, b * LANES:(b + 1) * LANES]

    def o_of(pi, h):
        if dils[pi] == 1:
            return o_in[pi][:, h * LANES:(h + 1) * LANES].astype(F32)
        return o_scr[pi, h]

    def l_of(pi, h):
        b, lane = h // heads_per_lse_block, h % heads_per_lse_block
        if dils[pi] == 1:
            return l_in[pi][:, b * LANES + lane:b * LANES + lane + 1]
        return l_scr[pi, b, :, lane:lane + 1]

    for h in range(N_HEADS_A):
        sl = slice(h * LANES, (h + 1) * LANES)
        l1, l2, l3 = l_of(0, h), l_of(1, h), l_of(2, h)
        m = jnp.maximum(jnp.maximum(l1, l2), l3)
        e1, e2, e3 = jnp.exp2(l1 - m), jnp.exp2(l2 - m), jnp.exp2(l3 - m)
        den = e1 + e2 + e3
        o = o_of(0, h) * (e1 / den) + o_of(1, h) * (e2 / den) + o_of(2, h) * (e3 / den)
        ms = jnp.mean(o * o, axis=-1, keepdims=True)
        o_ref[:, sl] = (o * lax.rsqrt(ms + RMS_EPS) * g_ref[h:h + 1, :]).astype(o_ref.dtype)
    gates = jax.nn.sigmoid(gl_ref[...])
    for h in range(N_HEADS_B):
        sl = slice(h * LANES, (h + 1) * LANES)
        o = (gates[:, 3 * h:3 * h + 1] * oc_ref[:, sl].astype(F32)
             + gates[:, 3 * h + 1:3 * h + 2] * os_ref[:, sl].astype(F32)
             + gates[:, 3 * h + 2:3 * h + 3] * ow_ref[:, sl].astype(F32))
        ms = jnp.mean(o * o, axis=-1, keepdims=True)
        hh = N_HEADS_A + h
        o_ref[:, hh * LANES:(hh + 1) * LANES] = (
            o * lax.rsqrt(ms + RMS_EPS) * g_ref[hh:hh + 1, :]).astype(o_ref.dtype)


def _combine(o_dil, lse_dil, o_cmp, o_slc, o_win, gl, out_gain, tm=256):
    s = o_cmp.shape[0]
    wa = N_HEADS_A * HEAD_DIM
    wl = (N_HEADS_A // 8) * LANES
    wide = pl.BlockSpec((tm, wa), lambda i: (i, 0))

    def dil_spec(dil, width):
        if dil == 1:
            return pl.BlockSpec((tm, width), lambda i: (i, 0))
        return pl.BlockSpec((dil, tm // dil, width), lambda i: (0, i, 0))

    dils = [dil for _, dil in DILATED_PATTERNS]
    return pl.pallas_call(
        functools.partial(_combine_kernel, tm=tm),
        grid=(s // tm,),
        in_specs=[dil_spec(dil, wa) for dil in dils] + [dil_spec(dil, wl) for dil in dils]
        + [wide, wide, wide, pl.BlockSpec((tm, LANES), lambda i: (i, 0)),
           pl.BlockSpec((N_HEADS_A + N_HEADS_B, HEAD_DIM), lambda i: (0, 0))],
        out_specs=pl.BlockSpec((tm, 2 * wa), lambda i: (i, 0)),
        out_shape=jax.ShapeDtypeStruct((s, 2 * wa), BF16),
        scratch_shapes=[pltpu.VMEM((len(dils), N_HEADS_A, tm, LANES), F32),
                        pltpu.VMEM((len(dils), wl // LANES, tm, LANES), F32)],
        compiler_params=_cparams(("parallel",)),
        name="combine_headnorm",
    )(*o_dil, *lse_dil, o_cmp, o_slc, o_win, gl, out_gain)


def _topk_rounds(work, k):
    vals = []
    for _ in range(k):
        mx = jnp.max(work, axis=0, keepdims=True)
        vals.append(mx)
        work = jnp.where(work >= mx, -BIG, work)
    return vals


def _peer_topk_kernel(q_ref, keys_ref, s1_ref, s2_ref, rows_ref, *, tm):
    half = N_KEYS
    dn = (((1,), (1,)), ((), ()))
    for hh in range(s1_ref.shape[0]):
        q0 = hh * 2 * half
        s1 = lax.dot_general(keys_ref[hh, 0], q_ref[:, q0:q0 + half], dn, preferred_element_type=F32)
        s2 = lax.dot_general(keys_ref[hh, 1], q_ref[:, q0 + half:q0 + 2 * half], dn, preferred_element_type=F32)
        v1 = _topk_rounds(s1, PEER_TOPK + 1)
        v2 = _topk_rounds(s2, PEER_TOPK + 1)
        v2m = jnp.concatenate(v2[:PEER_TOPK], axis=0)
        cand = jnp.concatenate([v1[a] + v2m for a in range(PEER_TOPK)], axis=0)
        top = _topk_rounds(cand, PEER_TOPK + 1)
        z = jnp.zeros((1, tm), F32)
        for tv in top[:PEER_TOPK]:
            z = z + jnp.exp(tv - top[0])
        s1_ref[hh] = s1
        s2_ref[hh] = s2
        next_sum = jnp.maximum(top[PEER_TOPK], jnp.maximum(v1[PEER_TOPK] + v2[0], v1[0] + v2[PEER_TOPK]))
        thr = 0.5 * (top[PEER_TOPK - 1] + next_sum)
        rows_ref[hh] = jnp.concatenate([thr, v1[0], v2[0], 1.0 / z] * 2, axis=0)


ROW_THR, ROW_MAX1, ROW_MAX2, ROW_INVZ = 0, 1, 2, 3


def _peer_topk(qp, keys, tm=256, hp=2):
    s = qp.shape[0]
    stat = jax.ShapeDtypeStruct((PEER_HEADS, N_KEYS, s), F32)
    stat_spec = pl.BlockSpec((hp, N_KEYS, tm), lambda n, h: (h, 0, n))
    return pl.pallas_call(
        functools.partial(_peer_topk_kernel, tm=tm),
        grid=(s // tm, PEER_HEADS // hp),
        in_specs=[pl.BlockSpec((tm, hp * 2 * N_KEYS), lambda n, h: (n, h)),
                  pl.BlockSpec((hp, 2, N_KEYS, N_KEYS), lambda n, h: (h, 0, 0, 0))],
        out_specs=[stat_spec, stat_spec, pl.BlockSpec((hp, 8, tm), lambda n, h: (h, 0, n))],
        out_shape=[stat, stat, jax.ShapeDtypeStruct((PEER_HEADS, 8, s), F32)],
        compiler_params=_cparams(("parallel", "arbitrary")),
        name="peer_topk",
    )(qp, keys)


def _peer_gates(s1_ref, s2_ref, rows_ref, i1_base, n_i1, tm, live):
    w = [jnp.zeros((N_KEYS, tm), F32) for _ in range(n_i1)]
    for h in range(PEER_HEADS):
        thr = rows_ref[h, ROW_THR:ROW_THR + 1, :]
        max1 = rows_ref[h, ROW_MAX1:ROW_MAX1 + 1, :]
        max2 = rows_ref[h, ROW_MAX2:ROW_MAX2 + 1, :]
        inv_z = rows_ref[h, ROW_INVZ:ROW_INVZ + 1, :] * live
        s2 = s2_ref[h]
        e2 = jnp.exp(s2 - max2)
        for ii in range(n_i1):
            s1row = s1_ref[h, pl.ds(i1_base + ii, 1), :]
            e1row = jnp.exp(s1row - max1) * inv_z
            w[ii] = w[ii] + jnp.where(s2 >= thr - s1row, e2, 0.0) * e1row
    return w


def _peer_dense_kernel(xn_ref, down_ref, upb_ref, upa_ref, s1_ref, s2_ref, rows_ref, o_ref, hida_scr, hidb_scr,
                       *, tm, te, n_tiles):
    j = pl.program_id(1)
    half = te // 2
    n_i1 = half // N_KEYS
    dn = (((1,), (1,)), ((), ()))

    @pl.when(j == 0)
    def _():
        o_ref[...] = jnp.zeros(o_ref.shape, F32)
        hidb_scr[...] = jnp.zeros(hidb_scr.shape, F32)

    jb = jnp.maximum(j - 1, 0)
    ja = jnp.minimum(j, n_tiles - 1)
    live_a = jnp.where(j < n_tiles, 1.0, 0.0)

    def mix(w, hid):
        act = jax.nn.gelu(hid)
        return jnp.concatenate([w[ii] * act[ii * N_KEYS:(ii + 1) * N_KEYS] for ii in range(n_i1)],
                               axis=0).astype(BF16)

    wb = _peer_gates(s1_ref, s2_ref, rows_ref, (2 * jb + 1) * n_i1, n_i1, tm, 1.0)
    hida_scr[...] = lax.dot_general(down_ref[0:half, :], xn_ref[...], dn, preferred_element_type=F32)
    o_ref[...] += jnp.dot(upb_ref[...], mix(wb, hidb_scr[...]), preferred_element_type=F32)
    wa = _peer_gates(s1_ref, s2_ref, rows_ref, (2 * ja) * n_i1, n_i1, tm, live_a)
    hidb_scr[...] = lax.dot_general(down_ref[half:te, :], xn_ref[...], dn, preferred_element_type=F32)
    o_ref[...] += jnp.dot(upa_ref[...], mix(wa, hida_scr[...]), preferred_element_type=F32)


def _peer_dense(xn, down, up_t, s1, s2, rows, tm, te):
    s, d = xn.shape
    n_tiles = down.shape[0] // te
    half = te // 2
    stat_spec = pl.BlockSpec((PEER_HEADS, N_KEYS, tm), lambda n, j: (0, 0, n))
    return pl.pallas_call(
        functools.partial(_peer_dense_kernel, tm=tm, te=te, n_tiles=n_tiles),
        grid=(s // tm, n_tiles + 1),
        in_specs=[pl.BlockSpec((tm, d), lambda n, j: (n, 0)),
                  pl.BlockSpec((te, d), lambda n, j: (jnp.minimum(j, n_tiles - 1), 0)),
                  pl.BlockSpec((d, half), lambda n, j: (0, jnp.maximum(2 * j - 1, 0))),
                  pl.BlockSpec((d, half), lambda n, j: (0, jnp.minimum(2 * j, 2 * n_tiles - 1))),
                  stat_spec, stat_spec,
                  pl.BlockSpec((PEER_HEADS, 8, tm), lambda n, j: (0, 0, n))],
        out_specs=pl.BlockSpec((d, tm), lambda n, j: (0, n)),
        out_shape=jax.ShapeDtypeStruct((d, s), F32),
        scratch_shapes=[pltpu.VMEM((half, tm), F32), pltpu.VMEM((half, tm), F32)],
        compiler_params=_cparams(("parallel", "arbitrary")),
        name="peer_dense",
    )(xn, down, up_t, up_t, s1, s2, rows)


def _add_t_kernel(a_ref, bt_ref, o_ref):
    for c in range(a_ref.shape[1] // LANES):
        sl = slice(c * LANES, (c + 1) * LANES)
        o_ref[:, sl] = a_ref[:, sl] + bt_ref[sl, :].T


def _add_transposed(a, b_t, tm=256):
    s, d = a.shape
    spec = pl.BlockSpec((tm, d), lambda i: (i, 0))
    return pl.pallas_call(
        _add_t_kernel, grid=(s // tm,),
        in_specs=[spec, pl.BlockSpec((d, tm), lambda i: (0, i))], out_specs=spec,
        out_shape=jax.ShapeDtypeStruct((s, d), F32),
        compiler_params=_cparams(("parallel",)), name="residual_add",
    )(a, b_t)


def _tile(n, pref):
    t = pref
    while n % t:
        t //= 2
    return t


def _layer(x, attn_norm_g, w_in, qk_gain_a, qk_gain_b, rel_bias, cmp_pos, cmp_w1, cmp_w2,
           out_norm_g, w_out, ffn_norm_g, peer_w_query, peer_sub_keys, peer_down, peer_up):
    s, d = x.shape
    scale = HEAD_DIM ** -0.5
    ones = jnp.ones((HEAD_DIM,), F32)
    zeros = jnp.zeros((HEAD_DIM,), F32)

    xn = _rmsnorm(x, attn_norm_g)
    q_scale = scale * LOG2E
    gain_a = jnp.concatenate([qk_gain_a[0] * q_scale] * 16 + [qk_gain_a[1]] * 16 + [ones] * 16)
    mode_a = jnp.concatenate([ones] * 32 + [zeros] * 16)
    gain_b = jnp.concatenate([qk_gain_b[0] * q_scale] * 16 + [ones] * 8 + [qk_gain_b[2]] * 4 + [ones] * 4
                             + [qk_gain_b[3]] * 4 + [ones] * 4)
    mode_b = jnp.concatenate([ones] * 16 + [zeros] * 8 + [ones] * 4 + [zeros] * 4 + [ones] * 4 + [zeros] * 4)
    n_gate = w_in.shape[1] - PROJ_COLS
    w_gate = jnp.pad(w_in[:, PROJ_COLS:], ((0, 0), (0, LANES - n_gate))).astype(BF16)
    tm = _tile(s, 1024)
    dilations = tuple(dil for _, dil in DILATED_PATTERNS)
    proj_a = _proj_headnorm(xn, w_in[:, :PROJ_A_COLS].astype(BF16), gain_a.astype(F32).reshape(1, -1),
                            mode_a.reshape(1, -1), tm, 512, dilations)
    proj = _proj_headnorm(xn, w_in[:, PROJ_A_COLS:PROJ_COLS].astype(BF16), gain_b.astype(F32).reshape(1, -1),
                          mode_b.reshape(1, -1), tm, 1024)[0]
    gl = _matmul(xn, w_gate, F32, tm, LANES)

    table_a = rel_bias[:N_HEADS_A] * LOG2E
    table_b = rel_bias[N_HEADS_A:] * LOG2E

    o_dil, lse_dil = [], []
    hb = 8
    for (window, dil), arr in zip(DILATED_PATTERNS, proj_a):
        arr = arr.reshape(dil, s // dil, PROJ_A_COLS)
        bias = _band_bias(table_a, window // dil, dil, 1).reshape(N_HEADS_A // hb, hb, BAND_TQ, 2 * BAND_TQ)
        o, lse = _banded_attention(
            arr, bias, n_hb=N_HEADS_A // hb, hb=hb, r=1, nprev=1,
            qcol=lambda hh: COL_QA // hb + hh, kcol=lambda hh: COL_KA // hb + hh, vcol=lambda hh: COL_VA // hb + hh,
            with_lse=True)
        o_dil.append(o[0] if dil == 1 else o)
        lse_dil.append(lse[0] if dil == 1 else lse)

    nprev_w = -(-(WIN_B - 1) // BAND_TQ)
    gw = 2
    bias_w = _band_bias(table_b, WIN_B - 1, 1, nprev_w).reshape(
        N_KV_B // gw, gw * Q_PER_KV_B, BAND_TQ, (nprev_w + 1) * BAND_TQ)
    o_win = _banded_attention(
        proj[None], bias_w, n_hb=N_KV_B // gw, hb=gw * Q_PER_KV_B, r=Q_PER_KV_B, nprev=nprev_w,
        qcol=lambda hh: COL_QB // (gw * Q_PER_KV_B) + hh, kcol=lambda hh: COL_KWIN // gw + hh,
        vcol=lambda hh: COL_VWIN // gw + hh, with_lse=False)[0][0]

    n_chunk = s // CMP_STRIDE
    n_cmp = (s - CMP_BLOCK) // CMP_STRIDE + 1
    raw = proj[:, COL_KCMP * HEAD_DIM:(COL_VCMP + N_KV_B) * HEAD_DIM]
    xt = raw.reshape(n_chunk, CMP_STRIDE, 2, N_KV_B, HEAD_DIM).transpose(2, 3, 1, 0, 4)
    kv_c = _compress(xt, cmp_w1.astype(BF16), cmp_w2.astype(BF16),
                     cmp_pos.reshape(2, 1, CMP_BLOCK * HEAD_DIM).astype(BF16),
                     qk_gain_b[1].reshape(1, HEAD_DIM).astype(F32))
    n_rows = CMP_PAD + n_chunk + 16
    kv_c = jnp.pad(kv_c[:, :, :n_cmp], ((0, 0), (0, 0), (CMP_PAD, n_rows - CMP_PAD - n_cmp), (0, 0)))
    kcp = kv_c[0]
    vcpt = kv_c[1].transpose(0, 2, 1)

    tl = np.arange(CMP_TQ)[None, :]
    il = np.arange(CMP_BAND)[:, None]
    rel_near = CMP_STRIDE * CMP_PAD - (CMP_BLOCK - 1) + tl - CMP_STRIDE * il
    bnear = _bias_of_rel(table_b, rel_near).reshape(N_KV_B, Q_PER_KV_B, CMP_BAND, CMP_TQ)
    bnear = bnear.transpose(0, 2, 1, 3).reshape(N_KV_B, CMP_BAND, Q_PER_KV_B * CMP_TQ)
    b31 = table_b[:, N_BUCKETS - 1].astype(F32)
    far_hi = b31.astype(BF16).astype(F32)
    far_lo = (b31 - far_hi).astype(BF16).astype(F32)
    bfar = jnp.stack([jnp.repeat(v.reshape(N_KV_B, Q_PER_KV_B), CMP_TQ, axis=1) for v in (far_hi, far_lo)], axis=1)
    bfar = jnp.pad(bfar, ((0, 0), (0, 6), (0, 0)))
    caug_np = np.zeros((n_rows, CMP_AUG), np.float32)
    caug_np[:, 0:2] = 1.0
    caug_np[:CMP_PAD, 2] = 1.0
    caug_np[np.arange(n_rows), 3 + np.arange(n_rows) // 16] = 1.0
    kc_aug = jnp.concatenate([kcp, jnp.broadcast_to(jnp.asarray(caug_np, BF16)[None], (N_KV_B, n_rows, CMP_AUG))],
                             axis=2)
    n_slc = s // SLC_BLOCK
    ratio = SLC_BLOCK // CMP_STRIDE
    tt_np = np.zeros((n_slc, n_rows), np.float32)
    for jblk in range(n_slc):
        for off, wgt in ((-1, 1.0), (0, 2.0), (1, 2.0), (2, 2.0), (3, 1.0)):
            i_c = ratio * jblk + off
            if 0 <= i_c < n_cmp:
                tt_np[jblk, CMP_PAD + i_c] = wgt
    tt = jnp.asarray(tt_np, BF16)
    o_cmp, mask = _cmp_select(proj, kc_aug, vcpt, bnear, bfar, tt)

    vt = proj[:, COL_VSLC * HEAD_DIM:(COL_VSLC + N_KV_B) * HEAD_DIM].reshape(s, N_KV_B, HEAD_DIM).transpose(1, 2, 0)
    vt = jnp.concatenate([vt, jnp.ones((N_KV_B, 16, s), BF16)], axis=1)
    kl = np.arange(CMP_TQ)[:, None]
    od = np.arange(SLC_NEAR)[:, None, None]
    rel_s = CMP_TQ * od + tl[None] - kl[None]
    btile = _bias_of_rel(table_b, rel_s).reshape(N_KV_B, Q_PER_KV_B, SLC_NEAR, CMP_TQ, CMP_TQ)
    bdiff = btile - (far_hi + far_lo).reshape(N_KV_B, Q_PER_KV_B, 1, 1, 1)
    bdiff = bdiff.transpose(0, 2, 3, 1, 4).reshape(N_KV_B, SLC_NEAR, CMP_TQ, Q_PER_KV_B * CMP_TQ)
    aug_np = np.zeros((s, SLC_AUG), np.float32)
    tk = CMP_TQ
    aug_np[np.arange(s), (np.arange(s) % tk) // SLC_BLOCK] = 1.0
    aug_np[:, tk // SLC_BLOCK:tk // SLC_BLOCK + 2] = 1.0
    k_slc = proj[:, COL_KSLC * HEAD_DIM:(COL_KSLC + N_KV_B) * HEAD_DIM].reshape(s, N_KV_B, HEAD_DIM)
    k_aug = jnp.concatenate([k_slc.transpose(1, 0, 2),
                             jnp.broadcast_to(jnp.asarray(aug_np, BF16)[None], (N_KV_B, s, SLC_AUG))], axis=2)
    o_slc = _slc_attention(proj, k_aug, vt, mask, bdiff, bfar)

    o_n = _combine(o_dil, lse_dil, o_cmp, o_slc, o_win, gl, out_norm_g.astype(F32))
    h = _matmul_residual(o_n, w_out.astype(BF16), x, tm, _tile(d, 1024))

    xn2 = _rmsnorm(h, ffn_norm_g)
    qp = _matmul(xn2, peer_w_query.astype(BF16), BF16, tm, 1024)
    s1, s2, rows = _peer_topk(qp, peer_sub_keys.astype(BF16))
    peer_t = _peer_dense(xn2, peer_down.astype(BF16), peer_up.T.astype(BF16), s1, s2, rows, _tile(s, 512), 512)
    return _add_transposed(h, peer_t)


def kernel(x, attn_norm_g, w_in, qk_gain_a, qk_gain_b, rel_bias, cmp_pos, cmp_w1, cmp_w2, out_norm_g, w_out,
           ffn_norm_g, peer_w_query, peer_sub_keys, peer_down, peer_up):
    b = x.shape[0]
    outs = [_layer(x[bi], attn_norm_g[0], w_in[0], qk_gain_a[0], qk_gain_b[0], rel_bias, cmp_pos[0], cmp_w1[0],
                   cmp_w2[0], out_norm_g[0], w_out[0], ffn_norm_g[0], peer_w_query[0], peer_sub_keys[0],
                   peer_down[0], peer_up[0]) for bi in range(b)]
    return jnp.stack(outs, axis=0)
```

```python
import functools
import math

import jax
import jax.numpy as jnp
import numpy as np
from jax import lax
from jax.experimental import pallas as pl
from jax.experimental.pallas import tpu as pltpu

F32 = jnp.float32
BF16 = jnp.bfloat16

HEAD_DIM = 128
LANES = 128
N_HEADS_A = 16
N_HEADS_B = 16
N_KV_B = 4
Q_PER_KV_B = 4
DILATED_PATTERNS = ((128, 1), (512, 4), (2048, 16))
CMP_BLOCK = 32
CMP_STRIDE = 16
SLC_BLOCK = 64
N_SLC = 16
WIN_B = 512
N_BUCKETS = 32
MAX_DISTANCE = 2048
PEER_HEADS = 8
N_KEYS = 128
PEER_TOPK = 16
RMS_EPS = 1e-6
NEG = -1e30
BIG = 3e38
LOG2E = math.log2(math.e)

COL_QA, COL_KA, COL_VA = 0, 16, 32
N_PROJ_A_HEADS = 48
COL_QB, COL_KCMP, COL_VCMP, COL_KSLC, COL_VSLC, COL_KWIN, COL_VWIN = 0, 16, 20, 24, 28, 32, 36
N_PROJ_B_HEADS = 40
PROJ_A_COLS = N_PROJ_A_HEADS * HEAD_DIM
PROJ_COLS = (N_PROJ_A_HEADS + N_PROJ_B_HEADS) * HEAD_DIM

BAND_TQ = 128
DIL_HB = 16
CMP_TQ = 256
CMP_PAD = 112
CMP_BAND = 128
SLC_NEAR = 8
SLC_QT = 2
SLC_AUG = 16
CMP_AUG = 128
VMEM_LIMIT = 56 * 1024 * 1024
MM_TM = 1024
MM_TN = 1024
PROJ_A_TN = 512
PEER_TM = 512
PEER_TE = 512


def _cparams(sem):
    return pltpu.CompilerParams(dimension_semantics=sem, vmem_limit_bytes=VMEM_LIMIT)


def _rel_bucket(dist):
    n = np.maximum(dist, 0)
    max_exact = N_BUCKETS // 2
    nf = np.maximum(n, 1).astype(np.float32)
    log_part = (np.log(nf / np.float32(max_exact)) / np.float32(math.log(MAX_DISTANCE / max_exact))
                * np.float32(N_BUCKETS - max_exact))
    large = np.minimum(max_exact + log_part.astype(np.int32), N_BUCKETS - 1)
    return np.where(n < max_exact, n, large).astype(np.int32)


def _bias_of_rel(table, rel):
    rel = np.asarray(rel)
    bucket = np.where(rel >= 0, _rel_bucket(rel), N_BUCKETS).reshape(-1)
    onehot = (jnp.asarray(bucket)[None, :] == jnp.arange(N_BUCKETS + 1)[:, None]).astype(F32)
    table = jnp.concatenate([table.astype(F32), jnp.full((table.shape[0], 1), NEG, F32)], axis=1)
    out = jnp.dot(table, onehot, precision=lax.Precision.HIGHEST)
    return out.reshape((table.shape[0],) + rel.shape)


def _rmsnorm_kernel(x_ref, g_ref, o_ref):
    x = x_ref[...]
    ms = jnp.mean(x * x, axis=-1, keepdims=True)
    o_ref[...] = (x * lax.rsqrt(ms + RMS_EPS) * g_ref[...]).astype(o_ref.dtype)


def _rmsnorm(x, g, tm=256):
    s, d = x.shape
    return pl.pallas_call(
        _rmsnorm_kernel,
        grid=(s // tm,),
        in_specs=[pl.BlockSpec((tm, d), lambda i: (i, 0)), pl.BlockSpec((1, d), lambda i: (0, 0))],
        out_specs=pl.BlockSpec((tm, d), lambda i: (i, 0)),
        out_shape=jax.ShapeDtypeStruct((s, d), BF16),
        compiler_params=_cparams(("parallel",)),
        name="rmsnorm",
    )(x, g.reshape(1, d).astype(F32))


def _proj_kernel(x_ref, w_ref, gain_ref, mode_ref, *refs, n_chunks, dilations, tm):
    outs = refs[:len(dilations)]
    acc = jnp.dot(x_ref[...], w_ref[...], preferred_element_type=F32)
    for c in range(n_chunks):
        sl = slice(c * LANES, (c + 1) * LANES)
        t = acc[:, sl]
        ms = jnp.mean(t * t, axis=-1, keepdims=True)
        mult = jnp.where(mode_ref[:, sl] > 0, lax.rsqrt(ms + RMS_EPS) * gain_ref[:, sl], 1.0)
        y = t * mult
        if len(dilations) > 1:
            y_scr = refs[len(dilations)]
            y_scr[c] = y
        for o_ref, dil in zip(outs, dilations):
            if dil == 1:
                o_ref[:, sl] = y.astype(o_ref.dtype)
            else:
                for r in range(dil):
                    o_ref[r, :, sl] = y_scr[c, pl.ds(r, tm // dil, stride=dil), :].astype(o_ref.dtype)


def _proj_headnorm(xn, w, gain, mode, tm, tn, dilations=(1,)):
    s, d = xn.shape
    n = w.shape[1]
    out_specs, out_shape = [], []
    for dil in dilations:
        if dil == 1:
            out_specs.append(pl.BlockSpec((tm, tn), lambda i, j: (i, j)))
            out_shape.append(jax.ShapeDtypeStruct((s, n), BF16))
        else:
            out_specs.append(pl.BlockSpec((dil, tm // dil, tn), lambda i, j: (0, i, j)))
            out_shape.append(jax.ShapeDtypeStruct((dil, s // dil, n), BF16))
    scratch = [pltpu.VMEM((tn // LANES, tm, LANES), F32)] if len(dilations) > 1 else []
    return pl.pallas_call(
        functools.partial(_proj_kernel, n_chunks=tn // LANES, dilations=tuple(dilations), tm=tm),
        grid=(s // tm, n // tn),
        in_specs=[pl.BlockSpec((tm, d), lambda i, j: (i, 0)),
                  pl.BlockSpec((d, tn), lambda i, j: (0, j)),
                  pl.BlockSpec((1, tn), lambda i, j: (0, j)),
                  pl.BlockSpec((1, tn), lambda i, j: (0, j))],
        out_specs=out_specs,
        out_shape=out_shape,
        scratch_shapes=scratch,
        compiler_params=_cparams(("parallel", "arbitrary")),
        name="proj_headnorm",
    )(xn, w, gain, mode)


def _matmul_kernel(x_ref, w_ref, o_ref):
    o_ref[...] = jnp.dot(x_ref[...], w_ref[...], preferred_element_type=F32).astype(o_ref.dtype)


def _matmul(x, w, out_dtype, tm, tn):
    s, d = x.shape
    n = w.shape[1]
    return pl.pallas_call(
        _matmul_kernel,
        grid=(s // tm, n // tn),
        in_specs=[pl.BlockSpec((tm, d), lambda i, j: (i, 0)), pl.BlockSpec((d, tn), lambda i, j: (0, j))],
        out_specs=pl.BlockSpec((tm, tn), lambda i, j: (i, j)),
        out_shape=jax.ShapeDtypeStruct((s, n), out_dtype),
        compiler_params=_cparams(("parallel", "arbitrary")),
        name="matmul",
    )(x, w)


def _matmul_res_kernel(x_ref, w_ref, r_ref, o_ref):
    o_ref[...] = r_ref[...] + jnp.dot(x_ref[...], w_ref[...], preferred_element_type=F32)


def _matmul_residual(x, w, res, tm, tn):
    s, d = x.shape
    n = w.shape[1]
    return pl.pallas_call(
        _matmul_res_kernel,
        grid=(s // tm, n // tn),
        in_specs=[pl.BlockSpec((tm, d), lambda i, j: (i, 0)),
                  pl.BlockSpec((d, tn), lambda i, j: (0, j)),
                  pl.BlockSpec((tm, tn), lambda i, j: (i, j))],
        out_specs=pl.BlockSpec((tm, tn), lambda i, j: (i, j)),
        out_shape=jax.ShapeDtypeStruct((s, n), F32),
        compiler_params=_cparams(("parallel", "arbitrary")),
        name="matmul_residual",
    )(x, w, res)


def _banded_kernel(*refs, hb, r, nprev, tq, with_lse):
    nk = nprev + 1
    q_ref = refs[0]
    k_refs = refs[1:1 + nk]
    v_refs = refs[1 + nk:1 + 2 * nk]
    bias_ref = refs[1 + 2 * nk]
    o_ref = refs[2 + 2 * nk]
    i = pl.program_id(2)
    span = nk * tq
    ng = hb // r
    q3 = jnp.stack([jnp.concatenate([q_ref[:, (g * r + rr) * LANES:(g * r + rr + 1) * LANES] for rr in range(r)],
                                    axis=0) for g in range(ng)], axis=0)
    parts = []
    for j in range(nk):
        k3 = jnp.stack([k_refs[j][:, g * LANES:(g + 1) * LANES] for g in range(ng)], axis=0)
        sj = jnp.einsum('gqd,gkd->gqk', q3, k3, preferred_element_type=F32)
        if j < nprev:
            sj = sj + jnp.where(i >= nprev - j, 0.0, NEG)
        parts.append(sj)
    s = jnp.concatenate(parts, axis=2) + bias_ref[...].reshape(ng, r * tq, span)
    m = jnp.max(s, axis=-1, keepdims=True)
    p = jnp.exp2(s - m)
    den = jnp.sum(p, axis=-1, keepdims=True)
    pb = p.astype(BF16)
    o = None
    for j in range(nk):
        v3 = jnp.stack([v_refs[j][:, g * LANES:(g + 1) * LANES] for g in range(ng)], axis=0)
        oj = jnp.einsum('gqk,gkd->gqd', pb[:, :, j * tq:(j + 1) * tq], v3, preferred_element_type=F32)
        o = oj if o is None else o + oj
    o = o / den
    for g in range(ng):
        for rr in range(r):
            h = g * r + rr
            o_ref[:, h * LANES:(h + 1) * LANES] = o[g, rr * tq:(rr + 1) * tq].astype(o_ref.dtype)
    if with_lse:
        lse_ref = refs[3 + 2 * nk]
        lse = m + jnp.log(den) * LOG2E
        lane = lax.broadcasted_iota(jnp.int32, (tq, LANES), 1)
        lse_mat = jnp.zeros((tq, LANES), F32)
        for g in range(ng):
            lse_mat = jnp.where(lane == g, lse[g], lse_mat)
        lse_ref[...] = lse_mat


def _banded_attention(arr, bias, *, n_hb, hb, r, nprev, qcol, kcol, vcol, with_lse):
    tq = BAND_TQ
    n_r, length = arr.shape[0], arr.shape[1]
    nk = nprev + 1
    qw = hb * LANES
    kw = (hb // r) * LANES

    def kmap(col, back):
        return lambda rr, hh, i: (rr, jnp.maximum(i - back, 0), col(hh))

    in_specs = [pl.BlockSpec((None, tq, qw), lambda rr, hh, i: (rr, i, qcol(hh)))]
    in_specs += [pl.BlockSpec((None, tq, kw), kmap(kcol, nprev - j)) for j in range(nk)]
    in_specs += [pl.BlockSpec((None, tq, kw), kmap(vcol, nprev - j)) for j in range(nk)]
    in_specs += [pl.BlockSpec((None, hb, tq, nk * tq), lambda rr, hh, i: (hh, 0, 0, 0))]
    out_specs = [pl.BlockSpec((None, tq, qw), lambda rr, hh, i: (rr, i, hh))]
    out_shape = [jax.ShapeDtypeStruct((n_r, length, n_hb * qw), BF16)]
    if with_lse:
        out_specs.append(pl.BlockSpec((None, tq, LANES), lambda rr, hh, i: (rr, i, hh)))
        out_shape.append(jax.ShapeDtypeStruct((n_r, length, n_hb * LANES), F32))
    return pl.pallas_call(
        functools.partial(_banded_kernel, hb=hb, r=r, nprev=nprev, tq=tq, with_lse=with_lse),
        grid=(n_r, n_hb, length // tq),
        in_specs=in_specs,
        out_specs=out_specs,
        out_shape=out_shape,
        compiler_params=_cparams(("parallel", "parallel", "arbitrary")),
        name="banded_attention",
    )(*([arr] * (1 + 2 * nk)), bias)


def _band_bias(table, max_dist, dist_scale, nprev):
    tq = BAND_TQ
    q_loc = np.arange(tq)[:, None]
    k_loc = np.arange((nprev + 1) * tq)[None, :] - nprev * tq
    rel = q_loc - k_loc
    return _bias_of_rel(table, np.where((rel >= 0) & (rel <= max_dist), rel * dist_scale, -1))


def _compress_kernel(x_ref, w1_ref, w2_ref, pos_ref, gain_ref, o_ref, *, n_chunk):
    half = CMP_STRIDE
    a = jnp.zeros((n_chunk, HEAD_DIM), F32)
    b = jnp.zeros((n_chunk, HEAD_DIM), F32)
    for c in range(half):
        xc = x_ref[c]
        a = a + jnp.dot(xc, w1_ref[c * HEAD_DIM:(c + 1) * HEAD_DIM, :], preferred_element_type=F32)
        b = b + jnp.dot(xc, w1_ref[(half + c) * HEAD_DIM:(half + c + 1) * HEAD_DIM, :],
                        preferred_element_type=F32)
    pos = jnp.broadcast_to(pos_ref[...], (8, CMP_BLOCK * HEAD_DIM))
    posterm = jnp.dot(pos, w1_ref[...], preferred_element_type=F32)[0:1, :]
    pre = a + pltpu.roll(b, n_chunk - 1, 0) + posterm
    hid = jax.nn.gelu(pre)
    out = jnp.dot(hid.astype(BF16), w2_ref[...], preferred_element_type=F32)
    ms = jnp.mean(out * out, axis=-1, keepdims=True)
    normed = out * lax.rsqrt(ms + RMS_EPS) * gain_ref[...]
    o_ref[...] = jnp.where(pl.program_id(0) == 0, normed, out).astype(o_ref.dtype)


def _compress(xt, w1, w2, pos, gain):
    n_chunk = xt.shape[3]
    return pl.pallas_call(
        functools.partial(_compress_kernel, n_chunk=n_chunk),
        grid=(2, N_KV_B),
        in_specs=[pl.BlockSpec((None, None, CMP_STRIDE, n_chunk, HEAD_DIM), lambda w, g: (w, g, 0, 0, 0)),
                  pl.BlockSpec((None, CMP_BLOCK * HEAD_DIM, HEAD_DIM), lambda w, g: (w, 0, 0)),
                  pl.BlockSpec((None, HEAD_DIM, HEAD_DIM), lambda w, g: (w, 0, 0)),
                  pl.BlockSpec((None, 1, CMP_BLOCK * HEAD_DIM), lambda w, g: (w, 0, 0)),
                  pl.BlockSpec((1, HEAD_DIM), lambda w, g: (0, 0))],
        out_specs=pl.BlockSpec((None, None, n_chunk, HEAD_DIM), lambda w, g: (w, g, 0, 0)),
        out_shape=jax.ShapeDtypeStruct((2, N_KV_B, n_chunk, HEAD_DIM), BF16),
        compiler_params=_cparams(("parallel", "parallel")),
        name="nsa_compress",
    )(xt, w1, w2, pos, gain)


def _cmp_select_kernel(q_ref, kc_ref, vct_ref, bnear_ref, bfar_ref, tt_ref, o_ref, mask_ref, p_scr, imp_scr, *, tq):
    n = pl.program_id(1)
    t0 = n * tq
    wide = Q_PER_KV_B * tq
    band_lo = pl.multiple_of(n * (tq // CMP_STRIDE), 16)
    t = t0 + lax.broadcasted_iota(jnp.int32, (1, tq), 1)
    valid = jnp.concatenate([t >= CMP_BLOCK - 1] * Q_PER_KV_B, axis=1)
    q_t = jnp.concatenate([q_ref[:, r * LANES:(r + 1) * LANES].astype(F32).T.astype(BF16)
                           for r in range(Q_PER_KV_B)], axis=1)
    ri = lax.broadcasted_iota(jnp.int32, (CMP_AUG, wide), 0)
    blocked = (ri == 2) | ((ri >= 3) & (ri - 3 >= n * (tq // CMP_STRIDE) // 16))
    aug = jnp.where(ri == 0, bfar_ref[0:1, :], jnp.where(ri == 1, bfar_ref[1:2, :],
                                                         jnp.where(blocked, NEG, 0.0)))
    rhs = jnp.concatenate([q_t, aug.astype(BF16)], axis=0)
    brow = lax.broadcasted_iota(jnp.int32, (CMP_BAND, wide), 0) + band_lo
    band_pen = jnp.where(brow >= CMP_PAD, 0.0, NEG)

    def attend(nr):
        s_far = jnp.dot(kc_ref[0:nr, :], rhs, preferred_element_type=F32)
        s_band = jnp.dot(kc_ref[pl.ds(band_lo, CMP_BAND), 0:HEAD_DIM], q_t, preferred_element_type=F32)
        s_band = s_band + bnear_ref[...] + band_pen
        m = jnp.maximum(jnp.max(s_far, axis=0, keepdims=True), jnp.max(s_band, axis=0, keepdims=True))
        p_far = jnp.exp2(s_far - m)
        p_band = jnp.exp2(s_band - m)
        den = jnp.sum(p_far, axis=0, keepdims=True) + jnp.sum(p_band, axis=0, keepdims=True)
        inv = jnp.where(valid, 1.0 / den, 0.0)
        p_scr[0:nr, :] = p_far * inv
        p_scr[pl.ds(band_lo, CMP_BAND), :] = p_band * inv
        p = p_scr[0:nr, :]
        o_t = jnp.dot(vct_ref[:, 0:nr], p.astype(BF16), preferred_element_type=F32)
        psum = p[:, 0:tq]
        for r in range(Q_PER_KV_B):
            o_ref[:, r * LANES:(r + 1) * LANES] = o_t[:, r * tq:(r + 1) * tq].T.astype(o_ref.dtype)
            if r:
                psum = psum + p[:, r * tq:(r + 1) * tq]
        p_hi = psum.astype(BF16)
        p_lo = (psum - p_hi.astype(F32)).astype(BF16)
        imp_scr[...] = (jnp.dot(tt_ref[:, 0:nr], p_hi, preferred_element_type=F32)
                        + jnp.dot(tt_ref[:, 0:nr], p_lo, preferred_element_type=F32))

    n_rows = kc_ref.shape[0]
    step = -(-(n_rows - CMP_BAND) // (4 * LANES)) * LANES
    caps = sorted({min(n_rows, CMP_BAND + k * step) for k in range(1, 5)})
    band_top = band_lo + CMP_BAND
    prev_cap = 0
    for cap in caps:
        pl.when((band_top > prev_cap) & (band_top <= cap))(functools.partial(attend, cap))
        prev_cap = cap
    imp = imp_scr[...]
    n_slc = imp.shape[0]
    jj = lax.broadcasted_iota(jnp.int32, (n_slc, tq), 0)
    cur = t // SLC_BLOCK
    allowed = jj <= cur
    score = jnp.where(jj == 0, 3e9,
                      jnp.where(jj == cur, 2e9,
                                jnp.where(jj == cur - 1, 1e9, jnp.where(allowed, imp, NEG))))
    work = score
    thr = jnp.zeros((1, tq), F32)
    for _ in range(N_SLC):
        thr = jnp.max(work, axis=0, keepdims=True)
        work = jnp.where(work >= thr, -BIG, work)
    mask_ref[...] = jnp.where((score >= thr) & allowed, 0.0, NEG)


def _cmp_select(proj, kc_aug, vcpt, bnear, bfar, tt):
    s = proj.shape[0]
    tq = CMP_TQ
    wide = Q_PER_KV_B * tq
    n_rows = kc_aug.shape[1]
    n_slc = tt.shape[0]
    return pl.pallas_call(
        functools.partial(_cmp_select_kernel, tq=tq),
        grid=(N_KV_B, s // tq),
        in_specs=[pl.BlockSpec((tq, Q_PER_KV_B * LANES), lambda g, n: (n, COL_QB // Q_PER_KV_B + g)),
                  pl.BlockSpec((None, n_rows, HEAD_DIM + CMP_AUG), lambda g, n: (g, 0, 0)),
                  pl.BlockSpec((None, HEAD_DIM, n_rows), lambda g, n: (g, 0, 0)),
                  pl.BlockSpec((None, CMP_BAND, wide), lambda g, n: (g, 0, 0)),
                  pl.BlockSpec((None, 8, wide), lambda g, n: (g, 0, 0)),
                  pl.BlockSpec((n_slc, n_rows), lambda g, n: (0, 0))],
        out_specs=[pl.BlockSpec((tq, Q_PER_KV_B * LANES), lambda g, n: (n, g)),
                   pl.BlockSpec((None, n_slc, tq), lambda g, n: (g, 0, n))],
        out_shape=[jax.ShapeDtypeStruct((s, N_HEADS_B * HEAD_DIM), BF16),
                   jax.ShapeDtypeStruct((N_KV_B, n_slc, s), F32)],
        scratch_shapes=[pltpu.VMEM((n_rows, wide), F32), pltpu.VMEM((n_slc, tq), F32)],
        compiler_params=_cparams(("parallel", "arbitrary")),
        name="nsa_cmp_select",
    )(proj, kc_aug, vcpt, bnear, bfar, tt)


def _slc_kernel(q_ref, k_ref, vt_ref, mask_ref, bdiff_ref, bfar_ref, o_ref,
                s_scr, p_scr, alpha_scr, m_scr, acc_scr, *, tq):
    qt = SLC_QT
    first = pl.program_id(1) * qt
    last = first + qt - 1
    tk = tq
    blocks_per_chunk = tk // SLC_BLOCK
    tile_w = Q_PER_KV_B * tq
    wide = qt * tile_w
    q_t = jnp.concatenate([q_ref[t * tq:(t + 1) * tq, r * LANES:(r + 1) * LANES].astype(F32).T.astype(BF16)
                           for t in range(qt) for r in range(Q_PER_KV_B)], axis=1)
    tail_rows = [jnp.concatenate([bfar_ref[0:1, :]] * qt, axis=1), jnp.concatenate([bfar_ref[1:2, :]] * qt, axis=1),
                 jnp.zeros((SLC_AUG - blocks_per_chunk - 2, wide), F32)]

    def scores(c, slot, near):
        pens = [jnp.where(c > first + t, NEG, 0.0) for t in range(qt)]
        c = jnp.minimum(c, last)
        k0 = pl.multiple_of(c * tk, tk)
        mrows = []
        for b in range(blocks_per_chunk):
            row = mask_ref[pl.ds(c * blocks_per_chunk + b, 1), :]
            mrows.append(jnp.concatenate([row[:, t * tq:(t + 1) * tq] + pens[t]
                                          for t in range(qt) for _ in range(Q_PER_KV_B)], axis=1))
        aug = jnp.concatenate(mrows + tail_rows, axis=0).astype(BF16)
        rhs = jnp.concatenate([q_t, aug], axis=0)
        s = jnp.dot(k_ref[pl.ds(k0, tk), :], rhs, preferred_element_type=F32)
        if near:
            s = jnp.concatenate(
                [s[:, t * tile_w:(t + 1) * tile_w] + bdiff_ref[jnp.clip(first + t - c, 0, SLC_NEAR - 1)]
                 for t in range(qt)], axis=1)
        s_scr[slot] = s

    def softmax_update(slot):
        s = s_scr[slot]
        m_old = m_scr[...]
        m_new = jnp.maximum(m_old, jnp.max(s, axis=0, keepdims=True))
        alpha = jnp.exp2(m_old - m_new)
        p = jnp.exp2(s - m_new)
        m_scr[...] = m_new
        alpha_scr[slot] = alpha
        p_scr[slot] = p.astype(BF16)

    def values(c, slot):
        k0 = pl.multiple_of(jnp.clip(c, 0, last) * tk, tk)
        acc_scr[...] = alpha_scr[slot] * acc_scr[...] + jnp.dot(
            vt_ref[:, pl.ds(k0, tk)], p_scr[slot], preferred_element_type=F32)

    m_scr[...] = jnp.full(m_scr.shape, -BIG, F32)
    acc_scr[...] = jnp.zeros(acc_scr.shape, F32)
    p_scr[1] = jnp.zeros(p_scr.shape[1:], BF16)
    alpha_scr[1] = jnp.ones(alpha_scr.shape[1:], F32)
    scores(0, 0, True)

    def body(i, near):
        c = 2 * i
        scores(c + 1, 1, near)
        softmax_update(0)
        values(c - 1, 1)
        scores(c + 2, 0, near)
        softmax_update(1)
        values(c, 0)

    def far_body(i, carry):
        body(i, False)
        return carry

    def near_body(i, carry):
        body(i, True)
        return carry

    n_pairs = (last + 2) // 2
    n_far = jnp.clip((first - (SLC_NEAR - 1)) // 2, 0, n_pairs)
    lax.fori_loop(0, n_far, far_body, 0)
    lax.fori_loop(n_far, n_pairs, near_body, 0)
    values(2 * n_pairs - 1, 1)
    o_t = acc_scr[0:HEAD_DIM, :] / acc_scr[HEAD_DIM:HEAD_DIM + 1, :]
    for t in range(qt):
        for r in range(Q_PER_KV_B):
            lanes = slice((t * Q_PER_KV_B + r) * tq, (t * Q_PER_KV_B + r + 1) * tq)
            o_ref[t * tq:(t + 1) * tq, r * LANES:(r + 1) * LANES] = o_t[:, lanes].T.astype(o_ref.dtype)


def _slc_attention(proj, k_aug, vt, mask, bdiff, bfar):
    s = proj.shape[0]
    tq = CMP_TQ
    rows = SLC_QT * tq
    n_slc = mask.shape[1]
    tile_w = Q_PER_KV_B * tq
    wide = SLC_QT * tile_w
    v_rows = vt.shape[1]
    return pl.pallas_call(
        functools.partial(_slc_kernel, tq=tq),
        grid=(N_KV_B, s // rows),
        in_specs=[pl.BlockSpec((rows, Q_PER_KV_B * LANES), lambda g, n: (n, COL_QB // Q_PER_KV_B + g)),
                  pl.BlockSpec((None, s, HEAD_DIM + SLC_AUG), lambda g, n: (g, 0, 0)),
                  pl.BlockSpec((None, v_rows, s), lambda g, n: (g, 0, 0)),
                  pl.BlockSpec((None, n_slc, rows), lambda g, n: (g, 0, n)),
                  pl.BlockSpec((None, SLC_NEAR, tq, tile_w), lambda g, n: (g, 0, 0, 0)),
                  pl.BlockSpec((None, 8, tile_w), lambda g, n: (g, 0, 0))],
        out_specs=pl.BlockSpec((rows, Q_PER_KV_B * LANES), lambda g, n: (n, g)),
        out_shape=jax.ShapeDtypeStruct((s, N_HEADS_B * HEAD_DIM), BF16),
        scratch_shapes=[pltpu.VMEM((2, tq, wide), F32),
                        pltpu.VMEM((2, tq, wide), BF16),
                        pltpu.VMEM((2, 1, wide), F32),
                        pltpu.VMEM((1, wide), F32),
                        pltpu.VMEM((v_rows, wide), F32)],
        compiler_params=_cparams(("parallel", "arbitrary")),
        name="nsa_selected",
    )(proj, k_aug, vt, mask, bdiff, bfar)


def _combine_kernel(o1_ref, o2_ref, o3_ref, l1_ref, l2_ref, l3_ref, oc_ref, os_ref, ow_ref, gl_ref, g_ref, o_ref,
                    o_scr, l_scr, *, tm):
    heads_per_lse_block = DIL_HB
    n_lse_blocks = N_HEADS_A // heads_per_lse_block
    dils = [dil for _, dil in DILATED_PATTERNS]
    o_in = [o1_ref, o2_ref, o3_ref]
    l_in = [l1_ref, l2_ref, l3_ref]
    for pi, dil in enumerate(dils):
        if dil == 1:
            continue
        rows = tm // dil
        for r in range(dil):
            for h in range(N_HEADS_A):
                o_scr[pi, h, pl.ds(r, rows, stride=dil), :] = o_in[pi][r, :, h * LANES:(h + 1) * LANES].astype(F32)
            for b in range(n_lse_blocks):
                l_scr[pi, b, pl.ds(r, rows, stride=dil), :] = l_in[pi][r, :, b * LANES:(b + 1) * LANES]

    def o_of(pi, h):
        if dils[pi] == 1:
            return o_in[pi][:, h * LANES:(h + 1) * LANES].astype(F32)
        return o_scr[pi, h]

    def l_of(pi, h):
        b, lane = h // heads_per_lse_block, h % heads_per_lse_block
        if dils[pi] == 1:
            return l_in[pi][:, b * LANES + lane:b * LANES + lane + 1]
        return l_scr[pi, b, :, lane:lane + 1]

    for h in range(N_HEADS_A):
        sl = slice(h * LANES, (h + 1) * LANES)
        l1, l2, l3 = l_of(0, h), l_of(1, h), l_of(2, h)
        m = jnp.maximum(jnp.maximum(l1, l2), l3)
        e1, e2, e3 = jnp.exp2(l1 - m), jnp.exp2(l2 - m), jnp.exp2(l3 - m)
        den = e1 + e2 + e3
        o = o_of(0, h) * (e1 / den) + o_of(1, h) * (e2 / den) + o_of(2, h) * (e3 / den)
        ms = jnp.mean(o * o, axis=-1, keepdims=True)
        o_ref[:, sl] = (o * lax.rsqrt(ms + RMS_EPS) * g_ref[h:h + 1, :]).astype(o_ref.dtype)
    gates = jax.nn.sigmoid(gl_ref[...])
    for h in range(N_HEADS_B):
        sl = slice(h * LANES, (h + 1) * LANES)
        o = (gates[:, 3 * h:3 * h + 1] * oc_ref[:, sl].astype(F32)
             + gates[:, 3 * h + 1:3 * h + 2] * os_ref[:, sl].astype(F32)
             + gates[:, 3 * h + 2:3 * h + 3] * ow_ref[:, sl].astype(F32))
        ms = jnp.mean(o * o, axis=-1, keepdims=True)
        hh = N_HEADS_A + h
        o_ref[:, hh * LANES:(hh + 1) * LANES] = (
            o * lax.rsqrt(ms + RMS_EPS) * g_ref[hh:hh + 1, :]).astype(o_ref.dtype)


def _combine(o_dil, lse_dil, o_cmp, o_slc, o_win, gl, out_gain, tm=256):
    s = o_cmp.shape[0]
    wa = N_HEADS_A * HEAD_DIM
    wl = (N_HEADS_A // DIL_HB) * LANES
    wide = pl.BlockSpec((tm, wa), lambda i: (i, 0))

    def dil_spec(dil, width):
        if dil == 1:
            return pl.BlockSpec((tm, width), lambda i: (i, 0))
        return pl.BlockSpec((dil, tm // dil, width), lambda i: (0, i, 0))

    dils = [dil for _, dil in DILATED_PATTERNS]
    return pl.pallas_call(
        functools.partial(_combine_kernel, tm=tm),
        grid=(s // tm,),
        in_specs=[dil_spec(dil, wa) for dil in dils] + [dil_spec(dil, wl) for dil in dils]
        + [wide, wide, wide, pl.BlockSpec((tm, LANES), lambda i: (i, 0)),
           pl.BlockSpec((N_HEADS_A + N_HEADS_B, HEAD_DIM), lambda i: (0, 0))],
        out_specs=pl.BlockSpec((tm, 2 * wa), lambda i: (i, 0)),
        out_shape=jax.ShapeDtypeStruct((s, 2 * wa), BF16),
        scratch_shapes=[pltpu.VMEM((len(dils), N_HEADS_A, tm, LANES), F32),
                        pltpu.VMEM((len(dils), wl // LANES, tm, LANES), F32)],
        compiler_params=_cparams(("parallel",)),
        name="combine_headnorm",
    )(*o_dil, *lse_dil, o_cmp, o_slc, o_win, gl, out_gain)


def _topk_rounds(work, k):
    vals = []
    for _ in range(k):
        mx = jnp.max(work, axis=0, keepdims=True)
        vals.append(mx)
        work = jnp.where(work >= mx, -BIG, work)
    return vals


def _peer_topk_kernel(q_ref, keys_ref, s1_ref, s2_ref, rows_ref, *, tm):
    half = N_KEYS
    dn = (((1,), (1,)), ((), ()))
    for hh in range(s1_ref.shape[0]):
        q0 = hh * 2 * half
        s1 = lax.dot_general(keys_ref[hh, 0], q_ref[:, q0:q0 + half], dn, preferred_element_type=F32)
        s2 = lax.dot_general(keys_ref[hh, 1], q_ref[:, q0 + half:q0 + 2 * half], dn, preferred_element_type=F32)
        v1 = _topk_rounds(s1, PEER_TOPK + 1)
        v2 = _topk_rounds(s2, PEER_TOPK + 1)
        v2m = jnp.concatenate(v2[:PEER_TOPK], axis=0)
        cand = jnp.concatenate([v1[a] + v2m for a in range(PEER_TOPK)], axis=0)
        top = _topk_rounds(cand, PEER_TOPK + 1)
        z = jnp.zeros((1, tm), F32)
        for tv in top[:PEER_TOPK]:
            z = z + jnp.exp(tv - top[0])
        s1_ref[hh] = s1
        s2_ref[hh] = s2
        next_sum = jnp.maximum(top[PEER_TOPK], jnp.maximum(v1[PEER_TOPK] + v2[0], v1[0] + v2[PEER_TOPK]))
        thr = 0.5 * (top[PEER_TOPK - 1] + next_sum)
        rows_ref[hh] = jnp.concatenate([thr, v1[0], v2[0], 1.0 / z] * 2, axis=0)


ROW_THR, ROW_MAX1, ROW_MAX2, ROW_INVZ = 0, 1, 2, 3


def _peer_topk(qp, keys, tm=256, hp=4):
    s = qp.shape[0]
    stat = jax.ShapeDtypeStruct((PEER_HEADS, N_KEYS, s), F32)
    stat_spec = pl.BlockSpec((hp, N_KEYS, tm), lambda n, h: (h, 0, n))
    return pl.pallas_call(
        functools.partial(_peer_topk_kernel, tm=tm),
        grid=(s // tm, PEER_HEADS // hp),
        in_specs=[pl.BlockSpec((tm, hp * 2 * N_KEYS), lambda n, h: (n, h)),
                  pl.BlockSpec((hp, 2, N_KEYS, N_KEYS), lambda n, h: (h, 0, 0, 0))],
        out_specs=[stat_spec, stat_spec, pl.BlockSpec((hp, 8, tm), lambda n, h: (h, 0, n))],
        out_shape=[stat, stat, jax.ShapeDtypeStruct((PEER_HEADS, 8, s), F32)],
        compiler_params=_cparams(("parallel", "arbitrary")),
        name="peer_topk",
    )(qp, keys)


def _peer_gates(s1_ref, s2_ref, rows_ref, i1_base, n_i1, tm, live):
    w = [jnp.zeros((N_KEYS, tm), F32) for _ in range(n_i1)]
    for h in range(PEER_HEADS):
        thr = rows_ref[h, ROW_THR:ROW_THR + 1, :]
        max1 = rows_ref[h, ROW_MAX1:ROW_MAX1 + 1, :]
        max2 = rows_ref[h, ROW_MAX2:ROW_MAX2 + 1, :]
        inv_z = rows_ref[h, ROW_INVZ:ROW_INVZ + 1, :] * live
        s2 = s2_ref[h]
        e2 = jnp.exp(s2 - max2)
        for ii in range(n_i1):
            s1row = s1_ref[h, pl.ds(i1_base + ii, 1), :]
            e1row = jnp.exp(s1row - max1) * inv_z
            w[ii] = w[ii] + jnp.where(s2 >= thr - s1row, e2, 0.0) * e1row
    return w


def _peer_dense_kernel(xn_ref, down_ref, upb_ref, upa_ref, s1_ref, s2_ref, rows_ref, o_ref, hida_scr, hidb_scr,
                       *, tm, te, n_tiles):
    j = pl.program_id(1)
    half = te // 2
    n_i1 = half // N_KEYS
    dn = (((1,), (1,)), ((), ()))

    @pl.when(j == 0)
    def _():
        o_ref[...] = jnp.zeros(o_ref.shape, F32)
        hidb_scr[...] = jnp.zeros(hidb_scr.shape, F32)

    jb = jnp.maximum(j - 1, 0)
    ja = jnp.minimum(j, n_tiles - 1)
    live_a = jnp.where(j < n_tiles, 1.0, 0.0)

    def mix(w, hid):
        act = jax.nn.gelu(hid)
        return jnp.concatenate([w[ii] * act[ii * N_KEYS:(ii + 1) * N_KEYS] for ii in range(n_i1)],
                               axis=0).astype(BF16)

    wb = _peer_gates(s1_ref, s2_ref, rows_ref, (2 * jb + 1) * n_i1, n_i1, tm, 1.0)
    hida_scr[...] = lax.dot_general(down_ref[0:half, :], xn_ref[...], dn, preferred_element_type=F32)
    o_ref[...] += jnp.dot(upb_ref[...], mix(wb, hidb_scr[...]), preferred_element_type=F32)
    wa = _peer_gates(s1_ref, s2_ref, rows_ref, (2 * ja) * n_i1, n_i1, tm, live_a)
    hidb_scr[...] = lax.dot_general(down_ref[half:te, :], xn_ref[...], dn, preferred_element_type=F32)
    o_ref[...] += jnp.dot(upa_ref[...], mix(wa, hida_scr[...]), preferred_element_type=F32)


def _peer_dense(xn, down, up_t, s1, s2, rows, tm, te):
    s, d = xn.shape
    n_tiles = down.shape[0] // te
    half = te // 2
    stat_spec = pl.BlockSpec((PEER_HEADS, N_KEYS, tm), lambda n, j: (0, 0, n))
    return pl.pallas_call(
        functools.partial(_peer_dense_kernel, tm=tm, te=te, n_tiles=n_tiles),
        grid=(s // tm, n_tiles + 1),
        in_specs=[pl.BlockSpec((tm, d), lambda n, j: (n, 0)),
                  pl.BlockSpec((te, d), lambda n, j: (jnp.minimum(j, n_tiles - 1), 0)),
                  pl.BlockSpec((d, half), lambda n, j: (0, jnp.maximum(2 * j - 1, 0))),
                  pl.BlockSpec((d, half), lambda n, j: (0, jnp.minimum(2 * j, 2 * n_tiles - 1))),
                  stat_spec, stat_spec,
                  pl.BlockSpec((PEER_HEADS, 8, tm), lambda n, j: (0, 0, n))],
        out_specs=pl.BlockSpec((d, tm), lambda n, j: (0, n)),
        out_shape=jax.ShapeDtypeStruct((d, s), F32),
        scratch_shapes=[pltpu.VMEM((half, tm), F32), pltpu.VMEM((half, tm), F32)],
        compiler_params=_cparams(("parallel", "arbitrary")),
        name="peer_dense",
    )(xn, down, up_t, up_t, s1, s2, rows)


def _add_t_kernel(a_ref, bt_ref, o_ref):
    for c in range(a_ref.shape[1] // LANES):
        sl = slice(c * LANES, (c + 1) * LANES)
        o_ref[:, sl] = a_ref[:, sl] + bt_ref[sl, :].T


def _add_transposed(a, b_t, tm=256):
    s, d = a.shape
    spec = pl.BlockSpec((tm, d), lambda i: (i, 0))
    return pl.pallas_call(
        _add_t_kernel, grid=(s // tm,),
        in_specs=[spec, pl.BlockSpec((d, tm), lambda i: (0, i))], out_specs=spec,
        out_shape=jax.ShapeDtypeStruct((s, d), F32),
        compiler_params=_cparams(("parallel",)), name="residual_add",
    )(a, b_t)


def _tile(n, pref):
    t = pref
    while n % t:
        t //= 2
    return t


def _layer(x, attn_norm_g, w_in, qk_gain_a, qk_gain_b, rel_bias, cmp_pos, cmp_w1, cmp_w2,
           out_norm_g, w_out, ffn_norm_g, peer_w_query, peer_sub_keys, peer_down, peer_up):
    s, d = x.shape
    scale = HEAD_DIM ** -0.5
    ones = jnp.ones((HEAD_DIM,), F32)
    zeros = jnp.zeros((HEAD_DIM,), F32)

    xn = _rmsnorm(x, attn_norm_g)
    q_scale = scale * LOG2E
    gain_a = jnp.concatenate([qk_gain_a[0] * q_scale] * 16 + [qk_gain_a[1]] * 16 + [ones] * 16)
    mode_a = jnp.concatenate([ones] * 32 + [zeros] * 16)
    gain_b = jnp.concatenate([qk_gain_b[0] * q_scale] * 16 + [ones] * 8 + [qk_gain_b[2]] * 4 + [ones] * 4
                             + [qk_gain_b[3]] * 4 + [ones] * 4)
    mode_b = jnp.concatenate([ones] * 16 + [zeros] * 8 + [ones] * 4 + [zeros] * 4 + [ones] * 4 + [zeros] * 4)
    n_gate = w_in.shape[1] - PROJ_COLS
    w_gate = jnp.pad(w_in[:, PROJ_COLS:], ((0, 0), (0, LANES - n_gate))).astype(BF16)
    tm = _tile(s, MM_TM)
    dilations = tuple(dil for _, dil in DILATED_PATTERNS)
    proj_a = _proj_headnorm(xn, w_in[:, :PROJ_A_COLS].astype(BF16), gain_a.astype(F32).reshape(1, -1),
                            mode_a.reshape(1, -1), tm, PROJ_A_TN, dilations)
    proj = _proj_headnorm(xn, w_in[:, PROJ_A_COLS:PROJ_COLS].astype(BF16), gain_b.astype(F32).reshape(1, -1),
                          mode_b.reshape(1, -1), tm, MM_TN)[0]
    gl = _matmul(xn, w_gate, F32, tm, LANES)

    table_a = rel_bias[:N_HEADS_A] * LOG2E
    table_b = rel_bias[N_HEADS_A:] * LOG2E

    o_dil, lse_dil = [], []
    hb = DIL_HB
    for (window, dil), arr in zip(DILATED_PATTERNS, proj_a):
        arr = arr.reshape(dil, s // dil, PROJ_A_COLS)
        bias = _band_bias(table_a, window // dil, dil, 1).reshape(N_HEADS_A // hb, hb, BAND_TQ, 2 * BAND_TQ)
        o, lse = _banded_attention(
            arr, bias, n_hb=N_HEADS_A // hb, hb=hb, r=1, nprev=1,
            qcol=lambda hh: COL_QA // hb + hh, kcol=lambda hh: COL_KA // hb + hh, vcol=lambda hh: COL_VA // hb + hh,
            with_lse=True)
        o_dil.append(o[0] if dil == 1 else o)
        lse_dil.append(lse[0] if dil == 1 else lse)

    nprev_w = -(-(WIN_B - 1) // BAND_TQ)
    gw = 2
    bias_w = _band_bias(table_b, WIN_B - 1, 1, nprev_w).reshape(
        N_KV_B // gw, gw * Q_PER_KV_B, BAND_TQ, (nprev_w + 1) * BAND_TQ)
    o_win = _banded_attention(
        proj[None], bias_w, n_hb=N_KV_B // gw, hb=gw * Q_PER_KV_B, r=Q_PER_KV_B, nprev=nprev_w,
        qcol=lambda hh: COL_QB // (gw * Q_PER_KV_B) + hh, kcol=lambda hh: COL_KWIN // gw + hh,
        vcol=lambda hh: COL_VWIN // gw + hh, with_lse=False)[0][0]

    n_chunk = s // CMP_STRIDE
    n_cmp = (s - CMP_BLOCK) // CMP_STRIDE + 1
    raw = proj[:, COL_KCMP * HEAD_DIM:(COL_VCMP + N_KV_B) * HEAD_DIM]
    xt = raw.reshape(n_chunk, CMP_STRIDE, 2, N_KV_B, HEAD_DIM).transpose(2, 3, 1, 0, 4)
    kv_c = _compress(xt, cmp_w1.astype(BF16), cmp_w2.astype(BF16),
                     cmp_pos.reshape(2, 1, CMP_BLOCK * HEAD_DIM).astype(BF16),
                     qk_gain_b[1].reshape(1, HEAD_DIM).astype(F32))
    n_rows = CMP_PAD + n_chunk + 16
    kv_c = jnp.pad(kv_c[:, :, :n_cmp], ((0, 0), (0, 0), (CMP_PAD, n_rows - CMP_PAD - n_cmp), (0, 0)))
    kcp = kv_c[0]
    vcpt = kv_c[1].transpose(0, 2, 1)

    tl = np.arange(CMP_TQ)[None, :]
    il = np.arange(CMP_BAND)[:, None]
    rel_near = CMP_STRIDE * CMP_PAD - (CMP_BLOCK - 1) + tl - CMP_STRIDE * il
    bnear = _bias_of_rel(table_b, rel_near).reshape(N_KV_B, Q_PER_KV_B, CMP_BAND, CMP_TQ)
    bnear = bnear.transpose(0, 2, 1, 3).reshape(N_KV_B, CMP_BAND, Q_PER_KV_B * CMP_TQ)
    b31 = table_b[:, N_BUCKETS - 1].astype(F32)
    far_hi = b31.astype(BF16).astype(F32)
    far_lo = (b31 - far_hi).astype(BF16).astype(F32)
    bfar = jnp.stack([jnp.repeat(v.reshape(N_KV_B, Q_PER_KV_B), CMP_TQ, axis=1) for v in (far_hi, far_lo)], axis=1)
    bfar = jnp.pad(bfar, ((0, 0), (0, 6), (0, 0)))
    caug_np = np.zeros((n_rows, CMP_AUG), np.float32)
    caug_np[:, 0:2] = 1.0
    caug_np[:CMP_PAD, 2] = 1.0
    caug_np[np.arange(n_rows), 3 + np.arange(n_rows) // 16] = 1.0
    kc_aug = jnp.concatenate([kcp, jnp.broadcast_to(jnp.asarray(caug_np, BF16)[None], (N_KV_B, n_rows, CMP_AUG))],
                             axis=2)
    n_slc = s // SLC_BLOCK
    ratio = SLC_BLOCK // CMP_STRIDE
    tt_np = np.zeros((n_slc, n_rows), np.float32)
    for jblk in range(n_slc):
        for off, wgt in ((-1, 1.0), (0, 2.0), (1, 2.0), (2, 2.0), (3, 1.0)):
            i_c = ratio * jblk + off
            if 0 <= i_c < n_cmp:
                tt_np[jblk, CMP_PAD + i_c] = wgt
    tt = jnp.asarray(tt_np, BF16)
    o_cmp, mask = _cmp_select(proj, kc_aug, vcpt, bnear, bfar, tt)

    vt = proj[:, COL_VSLC * HEAD_DIM:(COL_VSLC + N_KV_B) * HEAD_DIM].reshape(s, N_KV_B, HEAD_DIM).transpose(1, 2, 0)
    vt = jnp.concatenate([vt, jnp.ones((N_KV_B, 16, s), BF16)], axis=1)
    kl = np.arange(CMP_TQ)[:, None]
    od = np.arange(SLC_NEAR)[:, None, None]
    rel_s = CMP_TQ * od + tl[None] - kl[None]
    btile = _bias_of_rel(table_b, rel_s).reshape(N_KV_B, Q_PER_KV_B, SLC_NEAR, CMP_TQ, CMP_TQ)
    bdiff = btile - (far_hi + far_lo).reshape(N_KV_B, Q_PER_KV_B, 1, 1, 1)
    bdiff = bdiff.transpose(0, 2, 3, 1, 4).reshape(N_KV_B, SLC_NEAR, CMP_TQ, Q_PER_KV_B * CMP_TQ)
    aug_np = np.zeros((s, SLC_AUG), np.float32)
    tk = CMP_TQ
    aug_np[np.arange(s), (np.arange(s) % tk) // SLC_BLOCK] = 1.0
    aug_np[:, tk // SLC_BLOCK:tk // SLC_BLOCK + 2] = 1.0
    k_slc = proj[:, COL_KSLC * HEAD_DIM:(COL_KSLC + N_KV_B) * HEAD_DIM].reshape(s, N_KV_B, HEAD_DIM)
    k_aug = jnp.concatenate([k_slc.transpose(1, 0, 2),
                             jnp.broadcast_to(jnp.asarray(aug_np, BF16)[None], (N_KV_B, s, SLC_AUG))], axis=2)
    o_slc = _slc_attention(proj, k_aug, vt, mask, bdiff, bfar)

    o_n = _combine(o_dil, lse_dil, o_cmp, o_slc, o_win, gl, out_norm_g.astype(F32))
    h = _matmul_residual(o_n, w_out.astype(BF16), x, tm, _tile(d, MM_TN))

    xn2 = _rmsnorm(h, ffn_norm_g)
    qp = _matmul(xn2, peer_w_query.astype(BF16), BF16, tm, MM_TN)
    s1, s2, rows = _peer_topk(qp, peer_sub_keys.astype(BF16))
    peer_t = _peer_dense(xn2, peer_down.astype(BF16), peer_up.T.astype(BF16), s1, s2, rows, _tile(s, PEER_TM), PEER_TE)
    return _add_transposed(h, peer_t)


def kernel(x, attn_norm_g, w_in, qk_gain_a, qk_gain_b, rel_bias, cmp_pos, cmp_w1, cmp_w2, out_norm_g, w_out,
           ffn_norm_g, peer_w_query, peer_sub_keys, peer_down, peer_up):
    b = x.shape[0]
    outs = [_layer(x[bi], attn_norm_g[0], w_in[0], qk_gain_a[0], qk_gain_b[0], rel_bias, cmp_pos[0], cmp_w1[0],
                   cmp_w2[0], out_norm_g[0], w_out[0], ffn_norm_g[0], peer_w_query[0], peer_sub_keys[0],
                   peer_down[0], peer_up[0]) for bi in range(b)]
    return jnp.stack(outs, axis=0)
```

```python
import functools
import math

import jax
import jax.numpy as jnp
import numpy as np
from jax import lax
from jax.experimental import pallas as pl
from jax.experimental.pallas import tpu as pltpu

F32 = jnp.float32
BF16 = jnp.bfloat16

HEAD_DIM = 128
LANES = 128
N_HEADS_A = 16
N_HEADS_B = 16
N_KV_B = 4
Q_PER_KV_B = 4
DILATED_PATTERNS = ((128, 1), (512, 4), (2048, 16))
CMP_BLOCK = 32
CMP_STRIDE = 16
SLC_BLOCK = 64
N_SLC = 16
WIN_B = 512
N_BUCKETS = 32
MAX_DISTANCE = 2048
PEER_HEADS = 8
N_KEYS = 128
PEER_TOPK = 16
RMS_EPS = 1e-6
NEG = -1e30
BIG = 3e38
LOG2E = math.log2(math.e)

COL_QA, COL_KA, COL_VA = 0, 16, 32
N_PROJ_A_HEADS = 48
COL_QB, COL_KCMP, COL_VCMP, COL_KSLC, COL_VSLC, COL_KWIN, COL_VWIN = 0, 16, 20, 24, 28, 32, 36
N_PROJ_B_HEADS = 40
PROJ_A_COLS = N_PROJ_A_HEADS * HEAD_DIM
PROJ_COLS = (N_PROJ_A_HEADS + N_PROJ_B_HEADS) * HEAD_DIM

BAND_TQ = 128
DIL_HB = 16
CMP_TQ = 256
CMP_PAD = 112
CMP_BAND = 128
SLC_NEAR = 8
SLC_QT = 2
SLC_AUG = 16
CMP_AUG = 128
VMEM_LIMIT = 56 * 1024 * 1024
MM_TM = 1024
MM_TN = 1024
PROJ_A_TN = 512
PEER_TM = 512
PEER_TE = 512


def _cparams(sem):
    return pltpu.CompilerParams(dimension_semantics=sem, vmem_limit_bytes=VMEM_LIMIT)


def _rel_bucket(dist):
    n = np.maximum(dist, 0)
    max_exact = N_BUCKETS // 2
    nf = np.maximum(n, 1).astype(np.float32)
    log_part = (np.log(nf / np.float32(max_exact)) / np.float32(math.log(MAX_DISTANCE / max_exact))
                * np.float32(N_BUCKETS - max_exact))
    large = np.minimum(max_exact + log_part.astype(np.int32), N_BUCKETS - 1)
    return np.where(n < max_exact, n, large).astype(np.int32)


def _bias_of_rel(table, rel):
    rel = np.asarray(rel)
    bucket = np.where(rel >= 0, _rel_bucket(rel), N_BUCKETS).reshape(-1)
    onehot = (jnp.asarray(bucket)[None, :] == jnp.arange(N_BUCKETS + 1)[:, None]).astype(F32)
    table = jnp.concatenate([table.astype(F32), jnp.full((table.shape[0], 1), NEG, F32)], axis=1)
    out = jnp.dot(table, onehot, precision=lax.Precision.HIGHEST)
    return out.reshape((table.shape[0],) + rel.shape)


def _rmsnorm_kernel(x_ref, g_ref, o_ref):
    x = x_ref[...]
    ms = jnp.mean(x * x, axis=-1, keepdims=True)
    o_ref[...] = (x * lax.rsqrt(ms + RMS_EPS) * g_ref[...]).astype(o_ref.dtype)


def _rmsnorm(x, g, tm=256):
    s, d = x.shape
    return pl.pallas_call(
        _rmsnorm_kernel,
        grid=(s // tm,),
        in_specs=[pl.BlockSpec((tm, d), lambda i: (i, 0)), pl.BlockSpec((1, d), lambda i: (0, 0))],
        out_specs=pl.BlockSpec((tm, d), lambda i: (i, 0)),
        out_shape=jax.ShapeDtypeStruct((s, d), BF16),
        compiler_params=_cparams(("parallel",)),
        name="rmsnorm",
    )(x, g.reshape(1, d).astype(F32))


def _proj_kernel(x_ref, w_ref, gain_ref, mode_ref, *refs, n_chunks, dilations, tm):
    outs = refs[:len(dilations)]
    acc = jnp.dot(x_ref[...], w_ref[...], preferred_element_type=F32)
    for c in range(n_chunks):
        sl = slice(c * LANES, (c + 1) * LANES)
        t = acc[:, sl]
        ms = jnp.mean(t * t, axis=-1, keepdims=True)
        mult = jnp.where(mode_ref[:, sl] > 0, lax.rsqrt(ms + RMS_EPS) * gain_ref[:, sl], 1.0)
        y = t * mult
        if len(dilations) > 1:
            y_scr = refs[len(dilations)]
            y_scr[c] = y
        for o_ref, dil in zip(outs, dilations):
            if dil == 1:
                o_ref[:, sl] = y.astype(o_ref.dtype)
            else:
                for r in range(dil):
                    o_ref[r, :, sl] = y_scr[c, pl.ds(r, tm // dil, stride=dil), :].astype(o_ref.dtype)


def _proj_headnorm(xn, w, gain, mode, tm, tn, dilations=(1,)):
    s, d = xn.shape
    n = w.shape[1]
    out_specs, out_shape = [], []
    for dil in dilations:
        if dil == 1:
            out_specs.append(pl.BlockSpec((tm, tn), lambda i, j: (i, j)))
            out_shape.append(jax.ShapeDtypeStruct((s, n), BF16))
        else:
            out_specs.append(pl.BlockSpec((dil, tm // dil, tn), lambda i, j: (0, i, j)))
            out_shape.append(jax.ShapeDtypeStruct((dil, s // dil, n), BF16))
    scratch = [pltpu.VMEM((tn // LANES, tm, LANES), F32)] if len(dilations) > 1 else []
    return pl.pallas_call(
        functools.partial(_proj_kernel, n_chunks=tn // LANES, dilations=tuple(dilations), tm=tm),
        grid=(s // tm, n // tn),
        in_specs=[pl.BlockSpec((tm, d), lambda i, j: (i, 0)),
                  pl.BlockSpec((d, tn), lambda i, j: (0, j)),
                  pl.BlockSpec((1, tn), lambda i, j: (0, j)),
                  pl.BlockSpec((1, tn), lambda i, j: (0, j))],
        out_specs=out_specs,
        out_shape=out_shape,
        scratch_shapes=scratch,
        compiler_params=_cparams(("parallel", "arbitrary")),
        name="proj_headnorm",
    )(xn, w, gain, mode)


def _matmul_kernel(x_ref, w_ref, o_ref):
    o_ref[...] = jnp.dot(x_ref[...], w_ref[...], preferred_element_type=F32).astype(o_ref.dtype)


def _matmul(x, w, out_dtype, tm, tn):
    s, d = x.shape
    n = w.shape[1]
    return pl.pallas_call(
        _matmul_kernel,
        grid=(s // tm, n // tn),
        in_specs=[pl.BlockSpec((tm, d), lambda i, j: (i, 0)), pl.BlockSpec((d, tn), lambda i, j: (0, j))],
        out_specs=pl.BlockSpec((tm, tn), lambda i, j: (i, j)),
        out_shape=jax.ShapeDtypeStruct((s, n), out_dtype),
        compiler_params=_cparams(("parallel", "arbitrary")),
        name="matmul",
    )(x, w)


def _matmul_res_kernel(x_ref, w_ref, r_ref, o_ref):
    o_ref[...] = r_ref[...] + jnp.dot(x_ref[...], w_ref[...], preferred_element_type=F32)


def _matmul_residual(x, w, res, tm, tn):
    s, d = x.shape
    n = w.shape[1]
    return pl.pallas_call(
        _matmul_res_kernel,
        grid=(s // tm, n // tn),
        in_specs=[pl.BlockSpec((tm, d), lambda i, j: (i, 0)),
                  pl.BlockSpec((d, tn), lambda i, j: (0, j)),
                  pl.BlockSpec((tm, tn), lambda i, j: (i, j))],
        out_specs=pl.BlockSpec((tm, tn), lambda i, j: (i, j)),
        out_shape=jax.ShapeDtypeStruct((s, n), F32),
        compiler_params=_cparams(("parallel", "arbitrary")),
        name="matmul_residual",
    )(x, w, res)


def _banded_kernel(*refs, hb, r, nprev, tq, with_lse):
    nk = nprev + 1
    q_ref = refs[0]
    k_refs = refs[1:1 + nk]
    v_refs = refs[1 + nk:1 + 2 * nk]
    bias_ref = refs[1 + 2 * nk]
    o_ref = refs[2 + 2 * nk]
    i = pl.program_id(2)
    span = nk * tq
    ng = hb // r
    q3 = jnp.stack([jnp.concatenate([q_ref[:, (g * r + rr) * LANES:(g * r + rr + 1) * LANES] for rr in range(r)],
                                    axis=0) for g in range(ng)], axis=0)
    parts = []
    for j in range(nk):
        k3 = jnp.stack([k_refs[j][:, g * LANES:(g + 1) * LANES] for g in range(ng)], axis=0)
        sj = jnp.einsum('gqd,gkd->gqk', q3, k3, preferred_element_type=F32)
        if j < nprev:
            sj = sj + jnp.where(i >= nprev - j, 0.0, NEG)
        parts.append(sj)
    s = jnp.concatenate(parts, axis=2) + bias_ref[...].reshape(ng, r * tq, span)
    m = jnp.max(s, axis=-1, keepdims=True)
    p = jnp.exp2(s - m)
    den = jnp.sum(p, axis=-1, keepdims=True)
    pb = p.astype(BF16)
    o = None
    for j in range(nk):
        v3 = jnp.stack([v_refs[j][:, g * LANES:(g + 1) * LANES] for g in range(ng)], axis=0)
        oj = jnp.einsum('gqk,gkd->gqd', pb[:, :, j * tq:(j + 1) * tq], v3, preferred_element_type=F32)
        o = oj if o is None else o + oj
    o = o / den
    for g in range(ng):
        for rr in range(r):
            h = g * r + rr
            o_ref[:, h * LANES:(h + 1) * LANES] = o[g, rr * tq:(rr + 1) * tq].astype(o_ref.dtype)
    if with_lse:
        lse_ref = refs[3 + 2 * nk]
        lse = m + jnp.log(den) * LOG2E
        lane = lax.broadcasted_iota(jnp.int32, (tq, LANES), 1)
        lse_mat = jnp.zeros((tq, LANES), F32)
        for g in range(ng):
            lse_mat = jnp.where(lane == g, lse[g], lse_mat)
        lse_ref[...] = lse_mat


def _banded_attention(arr, bias, *, n_hb, hb, r, nprev, qcol, kcol, vcol, with_lse):
    tq = BAND_TQ
    n_r, length = arr.shape[0], arr.shape[1]
    nk = nprev + 1
    qw = hb * LANES
    kw = (hb // r) * LANES

    def kmap(col, back):
        return lambda rr, hh, i: (rr, jnp.maximum(i - back, 0), col(hh))

    in_specs = [pl.BlockSpec((None, tq, qw), lambda rr, hh, i: (rr, i, qcol(hh)))]
    in_specs += [pl.BlockSpec((None, tq, kw), kmap(kcol, nprev - j)) for j in range(nk)]
    in_specs += [pl.BlockSpec((None, tq, kw), kmap(vcol, nprev - j)) for j in range(nk)]
    in_specs += [pl.BlockSpec((None, hb, tq, nk * tq), lambda rr, hh, i: (hh, 0, 0, 0))]
    out_specs = [pl.BlockSpec((None, tq, qw), lambda rr, hh, i: (rr, i, hh))]
    out_shape = [jax.ShapeDtypeStruct((n_r, length, n_hb * qw), BF16)]
    if with_lse:
        out_specs.append(pl.BlockSpec((None, tq, LANES), lambda rr, hh, i: (rr, i, hh)))
        out_shape.append(jax.ShapeDtypeStruct((n_r, length, n_hb * LANES), F32))
    return pl.pallas_call(
        functools.partial(_banded_kernel, hb=hb, r=r, nprev=nprev, tq=tq, with_lse=with_lse),
        grid=(n_r, n_hb, length // tq),
        in_specs=in_specs,
        out_specs=out_specs,
        out_shape=out_shape,
        compiler_params=_cparams(("parallel", "parallel", "arbitrary")),
        name="banded_attention",
    )(*([arr] * (1 + 2 * nk)), bias)


def _band_bias(table, max_dist, dist_scale, nprev):
    tq = BAND_TQ
    q_loc = np.arange(tq)[:, None]
    k_loc = np.arange((nprev + 1) * tq)[None, :] - nprev * tq
    rel = q_loc - k_loc
    return _bias_of_rel(table, np.where((rel >= 0) & (rel <= max_dist), rel * dist_scale, -1))


def _compress_kernel(x_ref, w1_ref, w2_ref, pos_ref, gain_ref, o_ref, *, n_chunk):
    half = CMP_STRIDE
    a = jnp.zeros((n_chunk, HEAD_DIM), F32)
    b = jnp.zeros((n_chunk, HEAD_DIM), F32)
    for c in range(half):
        xc = x_ref[c]
        a = a + jnp.dot(xc, w1_ref[c * HEAD_DIM:(c + 1) * HEAD_DIM, :], preferred_element_type=F32)
        b = b + jnp.dot(xc, w1_ref[(half + c) * HEAD_DIM:(half + c + 1) * HEAD_DIM, :],
                        preferred_element_type=F32)
    pos = jnp.broadcast_to(pos_ref[...], (8, CMP_BLOCK * HEAD_DIM))
    posterm = jnp.dot(pos, w1_ref[...], preferred_element_type=F32)[0:1, :]
    pre = a + pltpu.roll(b, n_chunk - 1, 0) + posterm
    hid = jax.nn.gelu(pre)
    out = jnp.dot(hid.astype(BF16), w2_ref[...], preferred_element_type=F32)
    ms = jnp.mean(out * out, axis=-1, keepdims=True)
    normed = out * lax.rsqrt(ms + RMS_EPS) * gain_ref[...]
    o_ref[...] = jnp.where(pl.program_id(0) == 0, normed, out).astype(o_ref.dtype)


def _compress(xt, w1, w2, pos, gain):
    n_chunk = xt.shape[3]
    return pl.pallas_call(
        functools.partial(_compress_kernel, n_chunk=n_chunk),
        grid=(2, N_KV_B),
        in_specs=[pl.BlockSpec((None, None, CMP_STRIDE, n_chunk, HEAD_DIM), lambda w, g: (w, g, 0, 0, 0)),
                  pl.BlockSpec((None, CMP_BLOCK * HEAD_DIM, HEAD_DIM), lambda w, g: (w, 0, 0)),
                  pl.BlockSpec((None, HEAD_DIM, HEAD_DIM), lambda w, g: (w, 0, 0)),
                  pl.BlockSpec((None, 1, CMP_BLOCK * HEAD_DIM), lambda w, g: (w, 0, 0)),
                  pl.BlockSpec((1, HEAD_DIM), lambda w, g: (0, 0))],
        out_specs=pl.BlockSpec((None, None, n_chunk, HEAD_DIM), lambda w, g: (w, g, 0, 0)),
        out_shape=jax.ShapeDtypeStruct((2, N_KV_B, n_chunk, HEAD_DIM), BF16),
        compiler_params=_cparams(("parallel", "parallel")),
        name="nsa_compress",
    )(xt, w1, w2, pos, gain)


def _cmp_select_kernel(q_ref, kc_ref, vct_ref, bnear_ref, bfar_ref, tt_ref, o_ref, mask_ref, p_scr, imp_scr, *, tq):
    n = pl.program_id(1)
    t0 = n * tq
    wide = Q_PER_KV_B * tq
    band_lo = pl.multiple_of(n * (tq // CMP_STRIDE), 16)
    t = t0 + lax.broadcasted_iota(jnp.int32, (1, tq), 1)
    valid = jnp.concatenate([t >= CMP_BLOCK - 1] * Q_PER_KV_B, axis=1)
    q_t = jnp.concatenate([q_ref[:, r * LANES:(r + 1) * LANES].astype(F32).T.astype(BF16)
                           for r in range(Q_PER_KV_B)], axis=1)
    ri = lax.broadcasted_iota(jnp.int32, (CMP_AUG, wide), 0)
    blocked = (ri == 2) | ((ri >= 3) & (ri - 3 >= n * (tq // CMP_STRIDE) // 16))
    aug = jnp.where(ri == 0, bfar_ref[0:1, :], jnp.where(ri == 1, bfar_ref[1:2, :],
                                                         jnp.where(blocked, NEG, 0.0)))
    rhs = jnp.concatenate([q_t, aug.astype(BF16)], axis=0)
    brow = lax.broadcasted_iota(jnp.int32, (CMP_BAND, wide), 0) + band_lo
    band_pen = jnp.where(brow >= CMP_PAD, 0.0, NEG)

    def attend(nr):
        s_far = jnp.dot(kc_ref[0:nr, :], rhs, preferred_element_type=F32)
        s_band = jnp.dot(kc_ref[pl.ds(band_lo, CMP_BAND), 0:HEAD_DIM], q_t, preferred_element_type=F32)
        s_band = s_band + bnear_ref[...] + band_pen
        m = jnp.maximum(jnp.max(s_far, axis=0, keepdims=True), jnp.max(s_band, axis=0, keepdims=True))
        p_far = jnp.exp2(s_far - m)
        p_band = jnp.exp2(s_band - m)
        den = jnp.sum(p_far, axis=0, keepdims=True) + jnp.sum(p_band, axis=0, keepdims=True)
        inv = jnp.where(valid, 1.0 / den, 0.0)
        p_scr[0:nr, :] = p_far * inv
        p_scr[pl.ds(band_lo, CMP_BAND), :] = p_band * inv
        p = p_scr[0:nr, :]
        o_t = jnp.dot(vct_ref[:, 0:nr], p.astype(BF16), preferred_element_type=F32)
        psum = p[:, 0:tq]
        for r in range(Q_PER_KV_B):
            o_ref[:, r * LANES:(r + 1) * LANES] = o_t[:, r * tq:(r + 1) * tq].T.astype(o_ref.dtype)
            if r:
                psum = psum + p[:, r * tq:(r + 1) * tq]
        p_hi = psum.astype(BF16)
        p_lo = (psum - p_hi.astype(F32)).astype(BF16)
        imp_scr[...] = (jnp.dot(tt_ref[:, 0:nr], p_hi, preferred_element_type=F32)
                        + jnp.dot(tt_ref[:, 0:nr], p_lo, preferred_element_type=F32))

    n_rows = kc_ref.shape[0]
    step = -(-(n_rows - CMP_BAND) // (4 * LANES)) * LANES
    caps = sorted({min(n_rows, CMP_BAND + k * step) for k in range(1, 5)})
    band_top = band_lo + CMP_BAND
    prev_cap = 0
    for cap in caps:
        pl.when((band_top > prev_cap) & (band_top <= cap))(functools.partial(attend, cap))
        prev_cap = cap
    imp = imp_scr[...]
    n_slc = imp.shape[0]
    jj = lax.broadcasted_iota(jnp.int32, (n_slc, tq), 0)
    cur = t // SLC_BLOCK
    allowed = jj <= cur
    score = jnp.where(jj == 0, 3e9,
                      jnp.where(jj == cur, 2e9,
                                jnp.where(jj == cur - 1, 1e9, jnp.where(allowed, imp, NEG))))
    work = score
    thr = jnp.zeros((1, tq), F32)
    for _ in range(N_SLC):
        thr = jnp.max(work, axis=0, keepdims=True)
        work = jnp.where(work >= thr, -BIG, work)
    mask_ref[...] = jnp.where((score >= thr) & allowed, 0.0, NEG)


def _cmp_select(proj, kc_aug, vcpt, bnear, bfar, tt):
    s = proj.shape[0]
    tq = CMP_TQ
    wide = Q_PER_KV_B * tq
    n_rows = kc_aug.shape[1]
    n_slc = tt.shape[0]
    return pl.pallas_call(
        functools.partial(_cmp_select_kernel, tq=tq),
        grid=(N_KV_B, s // tq),
        in_specs=[pl.BlockSpec((tq, Q_PER_KV_B * LANES), lambda g, n: (n, COL_QB // Q_PER_KV_B + g)),
                  pl.BlockSpec((None, n_rows, HEAD_DIM + CMP_AUG), lambda g, n: (g, 0, 0)),
                  pl.BlockSpec((None, HEAD_DIM, n_rows), lambda g, n: (g, 0, 0)),
                  pl.BlockSpec((None, CMP_BAND, wide), lambda g, n: (g, 0, 0)),
                  pl.BlockSpec((None, 8, wide), lambda g, n: (g, 0, 0)),
                  pl.BlockSpec((n_slc, n_rows), lambda g, n: (0, 0))],
        out_specs=[pl.BlockSpec((tq, Q_PER_KV_B * LANES), lambda g, n: (n, g)),
                   pl.BlockSpec((None, n_slc, tq), lambda g, n: (g, 0, n))],
        out_shape=[jax.ShapeDtypeStruct((s, N_HEADS_B * HEAD_DIM), BF16),
                   jax.ShapeDtypeStruct((N_KV_B, n_slc, s), F32)],
        scratch_shapes=[pltpu.VMEM((n_rows, wide), F32), pltpu.VMEM((n_slc, tq), F32)],
        compiler_params=_cparams(("parallel", "arbitrary")),
        name="nsa_cmp_select",
    )(proj, kc_aug, vcpt, bnear, bfar, tt)


def _slc_kernel(q_ref, k_ref, vt_ref, mask_ref, bdiff_ref, bfar_ref, o_ref,
                s_scr, p_scr, alpha_scr, m_scr, acc_scr, *, tq):
    qt = SLC_QT
    first = pl.program_id(1) * qt
    last = first + qt - 1
    tk = tq
    blocks_per_chunk = tk // SLC_BLOCK
    tile_w = Q_PER_KV_B * tq
    wide = qt * tile_w
    q_t = jnp.concatenate([q_ref[t * tq:(t + 1) * tq, r * LANES:(r + 1) * LANES].astype(F32).T.astype(BF16)
                           for t in range(qt) for r in range(Q_PER_KV_B)], axis=1)
    tail_rows = [jnp.concatenate([bfar_ref[0:1, :]] * qt, axis=1), jnp.concatenate([bfar_ref[1:2, :]] * qt, axis=1),
                 jnp.zeros((SLC_AUG - blocks_per_chunk - 2, wide), F32)]

    def scores(c, slot, near):
        pens = [jnp.where(c > first + t, NEG, 0.0) for t in range(qt)]
        c = jnp.minimum(c, last)
        k0 = pl.multiple_of(c * tk, tk)
        mrows = []
        for b in range(blocks_per_chunk):
            row = mask_ref[pl.ds(c * blocks_per_chunk + b, 1), :]
            mrows.append(jnp.concatenate([row[:, t * tq:(t + 1) * tq] + pens[t]
                                          for t in range(qt) for _ in range(Q_PER_KV_B)], axis=1))
        aug = jnp.concatenate(mrows + tail_rows, axis=0).astype(BF16)
        rhs = jnp.concatenate([q_t, aug], axis=0)
        s = jnp.dot(k_ref[pl.ds(k0, tk), :], rhs, preferred_element_type=F32)
        if near:
            s = jnp.concatenate(
                [s[:, t * tile_w:(t + 1) * tile_w] + bdiff_ref[jnp.clip(first + t - c, 0, SLC_NEAR - 1)]
                 for t in range(qt)], axis=1)
        s_scr[slot] = s

    def softmax_update(slot):
        s = s_scr[slot]
        m_old = m_scr[...]
        m_new = jnp.maximum(m_old, jnp.max(s, axis=0, keepdims=True))
        alpha = jnp.exp2(m_old - m_new)
        p = jnp.exp2(s - m_new)
        m_scr[...] = m_new
        alpha_scr[slot] = alpha
        p_scr[slot] = p.astype(BF16)

    def values(c, slot):
        k0 = pl.multiple_of(jnp.clip(c, 0, last) * tk, tk)
        acc_scr[...] = alpha_scr[slot] * acc_scr[...] + jnp.dot(
            vt_ref[:, pl.ds(k0, tk)], p_scr[slot], preferred_element_type=F32)

    m_scr[...] = jnp.full(m_scr.shape, -BIG, F32)
    acc_scr[...] = jnp.zeros(acc_scr.shape, F32)
    p_scr[1] = jnp.zeros(p_scr.shape[1:], BF16)
    alpha_scr[1] = jnp.ones(alpha_scr.shape[1:], F32)
    scores(0, 0, True)

    def body(i, near):
        c = 2 * i
        scores(c + 1, 1, near)
        softmax_update(0)
        values(c - 1, 1)
        scores(c + 2, 0, near)
        softmax_update(1)
        values(c, 0)

    def far_body(i, carry):
        body(i, False)
        return carry

    def near_body(i, carry):
        body(i, True)
        return carry

    n_pairs = (last + 2) // 2
    n_far = jnp.clip((first - (SLC_NEAR - 1)) // 2, 0, n_pairs)
    lax.fori_loop(0, n_far, far_body, 0)
    lax.fori_loop(n_far, n_pairs, near_body, 0)
    values(2 * n_pairs - 1, 1)
    o_t = acc_scr[0:HEAD_DIM, :] / acc_scr[HEAD_DIM:HEAD_DIM + 1, :]
    for t in range(qt):
        for r in range(Q_PER_KV_B):
            lanes = slice((t * Q_PER_KV_B + r) * tq, (t * Q_PER_KV_B + r + 1) * tq)
            o_ref[t * tq:(t + 1) * tq, r * LANES:(r + 1) * LANES] = o_t[:, lanes].T.astype(o_ref.dtype)


def _slc_attention(proj, k_aug, vt, mask, bdiff, bfar):
    s = proj.shape[0]
    tq = CMP_TQ
    rows = SLC_QT * tq
    n_slc = mask.shape[1]
    tile_w = Q_PER_KV_B * tq
    wide = SLC_QT * tile_w
    v_rows = vt.shape[1]
    return pl.pallas_call(
        functools.partial(_slc_kernel, tq=tq),
        grid=(N_KV_B, s // rows),
        in_specs=[pl.BlockSpec((rows, Q_PER_KV_B * LANES), lambda g, n: (n, COL_QB // Q_PER_KV_B + g)),
                  pl.BlockSpec((None, s, HEAD_DIM + SLC_AUG), lambda g, n: (g, 0, 0)),
                  pl.BlockSpec((None, v_rows, s), lambda g, n: (g, 0, 0)),
                  pl.BlockSpec((None, n_slc, rows), lambda g, n: (g, 0, n)),
                  pl.BlockSpec((None, SLC_NEAR, tq, tile_w), lambda g, n: (g, 0, 0, 0)),
                  pl.BlockSpec((None, 8, tile_w), lambda g, n: (g, 0, 0))],
        out_specs=pl.BlockSpec((rows, Q_PER_KV_B * LANES), lambda g, n: (n, g)),
        out_shape=jax.ShapeDtypeStruct((s, N_HEADS_B * HEAD_DIM), BF16),
        scratch_shapes=[pltpu.VMEM((2, tq, wide), F32),
                        pltpu.VMEM((2, tq, wide), BF16),
                        pltpu.VMEM((2, 1, wide), F32),
                        pltpu.VMEM((1, wide), F32),
                        pltpu.VMEM((v_rows, wide), F32)],
        compiler_params=_cparams(("parallel", "arbitrary")),
        name="nsa_selected",
    )(proj, k_aug, vt, mask, bdiff, bfar)


def _combine_kernel(o1_ref, o2_ref, o3_ref, l1_ref, l2_ref, l3_ref, oc_ref, os_ref, ow_ref, gl_ref, g_ref, o_ref,
                    o_scr, l_scr, *, tm):
    heads_per_lse_block = DIL_HB
    n_lse_blocks = N_HEADS_A // heads_per_lse_block
    dils = [dil for _, dil in DILATED_PATTERNS]
    o_in = [o1_ref, o2_ref, o3_ref]
    l_in = [l1_ref, l2_ref, l3_ref]
    for pi, dil in enumerate(dils):
        if dil == 1:
            continue
        rows = tm // dil
        for r in range(dil):
            for h in range(N_HEADS_A):
                o_scr[pi, h, pl.ds(r, rows, stride=dil), :] = o_in[pi][r, :, h * LANES:(h + 1) * LANES].astype(F32)
            for b in range(n_lse_blocks):
                l_scr[pi, b, pl.ds(r, rows, stride=dil), :] = l_in[pi][r, :, b * LANES:(b + 1) * LANES]

    def o_of(pi, h):
        if dils[pi] == 1:
            return o_in[pi][:, h * LANES:(h + 1) * LANES].astype(F32)
        return o_scr[pi, h]

    def l_of(pi, h):
        b, lane = h // heads_per_lse_block, h % heads_per_lse_block
        if dils[pi] == 1:
            return l_in[pi][:, b * LANES + lane:b * LANES + lane + 1]
        return l_scr[pi, b, :, lane:lane + 1]

    for h in range(N_HEADS_A):
        sl = slice(h * LANES, (h + 1) * LANES)
        l1, l2, l3 = l_of(0, h), l_of(1, h), l_of(2, h)
        m = jnp.maximum(jnp.maximum(l1, l2), l3)
        e1, e2, e3 = jnp.exp2(l1 - m), jnp.exp2(l2 - m), jnp.exp2(l3 - m)
        den = e1 + e2 + e3
        o = o_of(0, h) * (e1 / den) + o_of(1, h) * (e2 / den) + o_of(2, h) * (e3 / den)
        ms = jnp.mean(o * o, axis=-1, keepdims=True)
        o_ref[:, sl] = (o * lax.rsqrt(ms + RMS_EPS) * g_ref[h:h + 1, :]).astype(o_ref.dtype)
    gates = jax.nn.sigmoid(gl_ref[...])
    for h in range(N_HEADS_B):
        sl = slice(h * LANES, (h + 1) * LANES)
        o = (gates[:, 3 * h:3 * h + 1] * oc_ref[:, sl].astype(F32)
             + gates[:, 3 * h + 1:3 * h + 2] * os_ref[:, sl].astype(F32)
             + gates[:, 3 * h + 2:3 * h + 3] * ow_ref[:, sl].astype(F32))
        ms = jnp.mean(o * o, axis=-1, keepdims=True)
        hh = N_HEADS_A + h
        o_ref[:, hh * LANES:(hh + 1) * LANES] = (
            o * lax.rsqrt(ms + RMS_EPS) * g_ref[hh:hh + 1, :]).astype(o_ref.dtype)


def _combine(o_dil, lse_dil, o_cmp, o_slc, o_win, gl, out_gain, tm=256):
    s = o_cmp.shape[0]
    wa = N_HEADS_A * HEAD_DIM
    wl = (N_HEADS_A // DIL_HB) * LANES
    wide = pl.BlockSpec((tm, wa), lambda i: (i, 0))

    def dil_spec(dil, width):
        if dil == 1:
            return pl.BlockSpec((tm, width), lambda i: (i, 0))
        return pl.BlockSpec((dil, tm // dil, width), lambda i: (0, i, 0))

    dils = [dil for _, dil in DILATED_PATTERNS]
    return pl.pallas_call(
        functools.partial(_combine_kernel, tm=tm),
        grid=(s // tm,),
        in_specs=[dil_spec(dil, wa) for dil in dils] + [dil_spec(dil, wl) for dil in dils]
        + [wide, wide, wide, pl.BlockSpec((tm, LANES), lambda i: (i, 0)),
           pl.BlockSpec((N_HEADS_A + N_HEADS_B, HEAD_DIM), lambda i: (0, 0))],
        out_specs=pl.BlockSpec((tm, 2 * wa), lambda i: (i, 0)),
        out_shape=jax.ShapeDtypeStruct((s, 2 * wa), BF16),
        scratch_shapes=[pltpu.VMEM((len(dils), N_HEADS_A, tm, LANES), F32),
                        pltpu.VMEM((len(dils), wl // LANES, tm, LANES), F32)],
        compiler_params=_cparams(("parallel",)),
        name="combine_headnorm",
    )(*o_dil, *lse_dil, o_cmp, o_slc, o_win, gl, out_gain)


def _topk_rounds(work, k):
    vals = []
    for _ in range(k):
        mx = jnp.max(work, axis=0, keepdims=True)
        vals.append(mx)
        work = jnp.where(work >= mx, -BIG, work)
    return vals


def _peer_topk_kernel(q_ref, keys_ref, s1_ref, s2_ref, rows_ref, *, tm):
    half = N_KEYS
    dn = (((1,), (1,)), ((), ()))
    for hh in range(s1_ref.shape[0]):
        q0 = hh * 2 * half
        s1 = lax.dot_general(keys_ref[hh, 0], q_ref[:, q0:q0 + half], dn, preferred_element_type=F32)
        s2 = lax.dot_general(keys_ref[hh, 1], q_ref[:, q0 + half:q0 + 2 * half], dn, preferred_element_type=F32)
        v1 = _topk_rounds(s1, PEER_TOPK + 1)
        v2 = _topk_rounds(s2, PEER_TOPK + 1)
        v2m = jnp.concatenate(v2[:PEER_TOPK], axis=0)
        cand = jnp.concatenate([v1[a] + v2m for a in range(PEER_TOPK)], axis=0)
        top = _topk_rounds(cand, PEER_TOPK + 1)
        z = jnp.zeros((1, tm), F32)
        for tv in top[:PEER_TOPK]:
            z = z + jnp.exp(tv - top[0])
        s1_ref[hh] = s1
        s2_ref[hh] = s2
        next_sum = jnp.maximum(top[PEER_TOPK], jnp.maximum(v1[PEER_TOPK] + v2[0], v1[0] + v2[PEER_TOPK]))
        thr = 0.5 * (top[PEER_TOPK - 1] + next_sum)
        rows_ref[hh] = jnp.concatenate([thr, v1[0], v2[0], 1.0 / z] * 2, axis=0)


ROW_THR, ROW_MAX1, ROW_MAX2, ROW_INVZ = 0, 1, 2, 3


def _peer_topk(qp, keys, tm=256, hp=4):
    s = qp.shape[0]
    stat = jax.ShapeDtypeStruct((PEER_HEADS, N_KEYS, s), F32)
    stat_spec = pl.BlockSpec((hp, N_KEYS, tm), lambda n, h: (h, 0, n))
    return pl.pallas_call(
        functools.partial(_peer_topk_kernel, tm=tm),
        grid=(s // tm, PEER_HEADS // hp),
        in_specs=[pl.BlockSpec((tm, hp * 2 * N_KEYS), lambda n, h: (n, h)),
                  pl.BlockSpec((hp, 2, N_KEYS, N_KEYS), lambda n, h: (h, 0, 0, 0))],
        out_specs=[stat_spec, stat_spec, pl.BlockSpec((hp, 8, tm), lambda n, h: (h, 0, n))],
        out_shape=[stat, stat, jax.ShapeDtypeStruct((PEER_HEADS, 8, s), F32)],
        compiler_params=_cparams(("parallel", "arbitrary")),
        name="peer_topk",
    )(qp, keys)


def _peer_gates(s1_ref, s2_ref, rows_ref, i1_base, n_i1, tm, live):
    w = [jnp.zeros((N_KEYS, tm), F32) for _ in range(n_i1)]
    for h in range(PEER_HEADS):
        thr = rows_ref[h, ROW_THR:ROW_THR + 1, :]
        max1 = rows_ref[h, ROW_MAX1:ROW_MAX1 + 1, :]
        max2 = rows_ref[h, ROW_MAX2:ROW_MAX2 + 1, :]
        inv_z = rows_ref[h, ROW_INVZ:ROW_INVZ + 1, :] * live
        s2 = s2_ref[h]
        e2 = jnp.exp(s2 - max2)
        for ii in range(n_i1):
            s1row = s1_ref[h, pl.ds(i1_base + ii, 1), :]
            e1row = jnp.exp(s1row - max1) * inv_z
            w[ii] = w[ii] + jnp.where(s2 >= thr - s1row, e2, 0.0) * e1row
    return w


def _peer_dense_kernel(xn_ref, down_ref, upb_ref, upa_ref, s1_ref, s2_ref, rows_ref, o_ref, hida_scr, hidb_scr,
                       *, tm, te, n_tiles):
    j = pl.program_id(1)
    half = te // 2
    n_i1 = half // N_KEYS
    dn = (((1,), (1,)), ((), ()))

    @pl.when(j == 0)
    def _():
        o_ref[...] = jnp.zeros(o_ref.shape, F32)
        hidb_scr[...] = jnp.zeros(hidb_scr.shape, F32)

    jb = jnp.maximum(j - 1, 0)
    ja = jnp.minimum(j, n_tiles - 1)
    live_a = jnp.where(j < n_tiles, 1.0, 0.0)

    def mix(w, hid):
        act = jax.nn.gelu(hid)
        return jnp.concatenate([w[ii] * act[ii * N_KEYS:(ii + 1) * N_KEYS] for ii in range(n_i1)],
                               axis=0).astype(BF16)

    wb = _peer_gates(s1_ref, s2_ref, rows_ref, (2 * jb + 1) * n_i1, n_i1, tm, 1.0)
    hida_scr[...] = lax.dot_general(down_ref[0:half, :], xn_ref[...], dn, preferred_element_type=F32)
    o_ref[...] += jnp.dot(upb_ref[...], mix(wb, hidb_scr[...]), preferred_element_type=F32)
    wa = _peer_gates(s1_ref, s2_ref, rows_ref, (2 * ja) * n_i1, n_i1, tm, live_a)
    hidb_scr[...] = lax.dot_general(down_ref[half:te, :], xn_ref[...], dn, preferred_element_type=F32)
    o_ref[...] += jnp.dot(upa_ref[...], mix(wa, hida_scr[...]), preferred_element_type=F32)


def _peer_dense(xn, down, up_t, s1, s2, rows, tm, te):
    s, d = xn.shape
    n_tiles = down.shape[0] // te
    half = te // 2
    stat_spec = pl.BlockSpec((PEER_HEADS, N_KEYS, tm), lambda n, j: (0, 0, n))
    return pl.pallas_call(
        functools.partial(_peer_dense_kernel, tm=tm, te=te, n_tiles=n_tiles),
        grid=(s // tm, n_tiles + 1),
        in_specs=[pl.BlockSpec((tm, d), lambda n, j: (n, 0)),
                  pl.BlockSpec((te, d), lambda n, j: (jnp.minimum(j, n_tiles - 1), 0)),
                  pl.BlockSpec((d, half), lambda n, j: (0, jnp.maximum(2 * j - 1, 0))),
                  pl.BlockSpec((d, half), lambda n, j: (0, jnp.minimum(2 * j, 2 * n_tiles - 1))),
                  stat_spec, stat_spec,
                  pl.BlockSpec((PEER_HEADS, 8, tm), lambda n, j: (0, 0, n))],
        out_specs=pl.BlockSpec((d, tm), lambda n, j: (0, n)),
        out_shape=jax.ShapeDtypeStruct((d, s), F32),
        scratch_shapes=[pltpu.VMEM((half, tm), F32), pltpu.VMEM((half, tm), F32)],
        compiler_params=_cparams(("parallel", "arbitrary")),
        name="peer_dense",
    )(xn, down, up_t, up_t, s1, s2, rows)


def _add_t_kernel(a_ref, bt_ref, o_ref):
    for c in range(a_ref.shape[1] // LANES):
        sl = slice(c * LANES, (c + 1) * LANES)
        o_ref[:, sl] = a_ref[:, sl] + bt_ref[sl, :].T


def _add_transposed(a, b_t, tm=256):
    s, d = a.shape
    spec = pl.BlockSpec((tm, d), lambda i: (i, 0))
    return pl.pallas_call(
        _add_t_kernel, grid=(s // tm,),
        in_specs=[spec, pl.BlockSpec((d, tm), lambda i: (0, i))], out_specs=spec,
        out_shape=jax.ShapeDtypeStruct((s, d), F32),
        compiler_params=_cparams(("parallel",)), name="residual_add",
    )(a, b_t)


def _tile(n, pref):
    t = pref
    while n % t:
        t //= 2
    return t


def _layer(x, attn_norm_g, w_in, qk_gain_a, qk_gain_b, rel_bias, cmp_pos, cmp_w1, cmp_w2,
           out_norm_g, w_out, ffn_norm_g, peer_w_query, peer_sub_keys, peer_down, peer_up):
    s, d = x.shape
    scale = HEAD_DIM ** -0.5
    ones = jnp.ones((HEAD_DIM,), F32)
    zeros = jnp.zeros((HEAD_DIM,), F32)

    xn = _rmsnorm(x, attn_norm_g)
    q_scale = scale * LOG2E
    gain_a = jnp.concatenate([qk_gain_a[0] * q_scale] * 16 + [qk_gain_a[1]] * 16 + [ones] * 16)
    mode_a = jnp.concatenate([ones] * 32 + [zeros] * 16)
    gain_b = jnp.concatenate([qk_gain_b[0] * q_scale] * 16 + [ones] * 8 + [qk_gain_b[2]] * 4 + [ones] * 4
                             + [qk_gain_b[3]] * 4 + [ones] * 4)
    mode_b = jnp.concatenate([ones] * 16 + [zeros] * 8 + [ones] * 4 + [zeros] * 4 + [ones] * 4 + [zeros] * 4)
    n_gate = w_in.shape[1] - PROJ_COLS
    w_gate = jnp.pad(w_in[:, PROJ_COLS:], ((0, 0), (0, LANES - n_gate))).astype(BF16)
    tm = _tile(s, MM_TM)
    dilations = tuple(dil for _, dil in DILATED_PATTERNS)
    proj_a = _proj_headnorm(xn, w_in[:, :PROJ_A_COLS].astype(BF16), gain_a.astype(F32).reshape(1, -1),
                            mode_a.reshape(1, -1), tm, PROJ_A_TN, dilations)
    proj = _proj_headnorm(xn, w_in[:, PROJ_A_COLS:PROJ_COLS].astype(BF16), gain_b.astype(F32).reshape(1, -1),
                          mode_b.reshape(1, -1), tm, MM_TN)[0]
    gl = _matmul(xn, w_gate, F32, tm, LANES)

    table_a = rel_bias[:N_HEADS_A] * LOG2E
    table_b = rel_bias[N_HEADS_A:] * LOG2E

    o_dil, lse_dil = [], []
    hb = DIL_HB
    for (window, dil), arr in zip(DILATED_PATTERNS, proj_a):
        arr = arr.reshape(dil, s // dil, PROJ_A_COLS)
        bias = _band_bias(table_a, window // dil, dil, 1).reshape(N_HEADS_A // hb, hb, BAND_TQ, 2 * BAND_TQ)
        o, lse = _banded_attention(
            arr, bias, n_hb=N_HEADS_A // hb, hb=hb, r=1, nprev=1,
            qcol=lambda hh: COL_QA // hb + hh, kcol=lambda hh: COL_KA // hb + hh, vcol=lambda hh: COL_VA // hb + hh,
            with_lse=True)
        o_dil.append(o[0] if dil == 1 else o)
        lse_dil.append(lse[0] if dil == 1 else lse)

    nprev_w = -(-(WIN_B - 1) // BAND_TQ)
    gw = N_KV_B
    bias_w = _band_bias(table_b, WIN_B - 1, 1, nprev_w).reshape(
        N_KV_B // gw, gw * Q_PER_KV_B, BAND_TQ, (nprev_w + 1) * BAND_TQ)
    o_win = _banded_attention(
        proj[None], bias_w, n_hb=N_KV_B // gw, hb=gw * Q_PER_KV_B, r=Q_PER_KV_B, nprev=nprev_w,
        qcol=lambda hh: COL_QB // (gw * Q_PER_KV_B) + hh, kcol=lambda hh: COL_KWIN // gw + hh,
        vcol=lambda hh: COL_VWIN // gw + hh, with_lse=False)[0][0]

    n_chunk = s // CMP_STRIDE
    n_cmp = (s - CMP_BLOCK) // CMP_STRIDE + 1
    raw = proj[:, COL_KCMP * HEAD_DIM:(COL_VCMP + N_KV_B) * HEAD_DIM]
    xt = raw.reshape(n_chunk, CMP_STRIDE, 2, N_KV_B, HEAD_DIM).transpose(2, 3, 1, 0, 4)
    kv_c = _compress(xt, cmp_w1.astype(BF16), cmp_w2.astype(BF16),
                     cmp_pos.reshape(2, 1, CMP_BLOCK * HEAD_DIM).astype(BF16),
                     qk_gain_b[1].reshape(1, HEAD_DIM).astype(F32))
    n_rows = CMP_PAD + n_chunk + 16
    kv_c = jnp.pad(kv_c[:, :, :n_cmp], ((0, 0), (0, 0), (CMP_PAD, n_rows - CMP_PAD - n_cmp), (0, 0)))
    kcp = kv_c[0]
    vcpt = kv_c[1].transpose(0, 2, 1)

    tl = np.arange(CMP_TQ)[None, :]
    il = np.arange(CMP_BAND)[:, None]
    rel_near = CMP_STRIDE * CMP_PAD - (CMP_BLOCK - 1) + tl - CMP_STRIDE * il
    bnear = _bias_of_rel(table_b, rel_near).reshape(N_KV_B, Q_PER_KV_B, CMP_BAND, CMP_TQ)
    bnear = bnear.transpose(0, 2, 1, 3).reshape(N_KV_B, CMP_BAND, Q_PER_KV_B * CMP_TQ)
    b31 = table_b[:, N_BUCKETS - 1].astype(F32)
    far_hi = b31.astype(BF16).astype(F32)
    far_lo = (b31 - far_hi).astype(BF16).astype(F32)
    bfar = jnp.stack([jnp.repeat(v.reshape(N_KV_B, Q_PER_KV_B), CMP_TQ, axis=1) for v in (far_hi, far_lo)], axis=1)
    bfar = jnp.pad(bfar, ((0, 0), (0, 6), (0, 0)))
    caug_np = np.zeros((n_rows, CMP_AUG), np.float32)
    caug_np[:, 0:2] = 1.0
    caug_np[:CMP_PAD, 2] = 1.0
    caug_np[np.arange(n_rows), 3 + np.arange(n_rows) // 16] = 1.0
    kc_aug = jnp.concatenate([kcp, jnp.broadcast_to(jnp.asarray(caug_np, BF16)[None], (N_KV_B, n_rows, CMP_AUG))],
                             axis=2)
    n_slc = s // SLC_BLOCK
    ratio = SLC_BLOCK // CMP_STRIDE
    tt_np = np.zeros((n_slc, n_rows), np.float32)
    for jblk in range(n_slc):
        for off, wgt in ((-1, 1.0), (0, 2.0), (1, 2.0), (2, 2.0), (3, 1.0)):
            i_c = ratio * jblk + off
            if 0 <= i_c < n_cmp:
                tt_np[jblk, CMP_PAD + i_c] = wgt
    tt = jnp.asarray(tt_np, BF16)
    o_cmp, mask = _cmp_select(proj, kc_aug, vcpt, bnear, bfar, tt)

    vt = proj[:, COL_VSLC * HEAD_DIM:(COL_VSLC + N_KV_B) * HEAD_DIM].reshape(s, N_KV_B, HEAD_DIM).transpose(1, 2, 0)
    vt = jnp.concatenate([vt, jnp.ones((N_KV_B, 16, s), BF16)], axis=1)
    kl = np.arange(CMP_TQ)[:, None]
    od = np.arange(SLC_NEAR)[:, None, None]
    rel_s = CMP_TQ * od + tl[None] - kl[None]
    btile = _bias_of_rel(table_b, rel_s).reshape(N_KV_B, Q_PER_KV_B, SLC_NEAR, CMP_TQ, CMP_TQ)
    bdiff = btile - (far_hi + far_lo).reshape(N_KV_B, Q_PER_KV_B, 1, 1, 1)
    bdiff = bdiff.transpose(0, 2, 3, 1, 4).reshape(N_KV_B, SLC_NEAR, CMP_TQ, Q_PER_KV_B * CMP_TQ)
    aug_np = np.zeros((s, SLC_AUG), np.float32)
    tk = CMP_TQ
    aug_np[np.arange(s), (np.arange(s) % tk) // SLC_BLOCK] = 1.0
    aug_np[:, tk // SLC_BLOCK:tk // SLC_BLOCK + 2] = 1.0
    k_slc = proj[:, COL_KSLC * HEAD_DIM:(COL_KSLC + N_KV_B) * HEAD_DIM].reshape(s, N_KV_B, HEAD_DIM)
    k_aug = jnp.concatenate([k_slc.transpose(1, 0, 2),
                             jnp.broadcast_to(jnp.asarray(aug_np, BF16)[None], (N_KV_B, s, SLC_AUG))], axis=2)
    o_slc = _slc_attention(proj, k_aug, vt, mask, bdiff, bfar)

    o_n = _combine(o_dil, lse_dil, o_cmp, o_slc, o_win, gl, out_norm_g.astype(F32))
    h = _matmul_residual(o_n, w_out.astype(BF16), x, tm, _tile(d, MM_TN))

    xn2 = _rmsnorm(h, ffn_norm_g)
    qp = _matmul(xn2, peer_w_query.astype(BF16), BF16, tm, MM_TN)
    s1, s2, rows = _peer_topk(qp, peer_sub_keys.astype(BF16))
    peer_t = _peer_dense(xn2, peer_down.astype(BF16), peer_up.T.astype(BF16), s1, s2, rows, _tile(s, PEER_TM), PEER_TE)
    return _add_transposed(h, peer_t)


def kernel(x, attn_norm_g, w_in, qk_gain_a, qk_gain_b, rel_bias, cmp_pos, cmp_w1, cmp_w2, out_norm_g, w_out,
           ffn_norm_g, peer_w_query, peer_sub_keys, peer_down, peer_up):
    b = x.shape[0]
    outs = [_layer(x[bi], attn_norm_g[0], w_in[0], qk_gain_a[0], qk_gain_b[0], rel_bias, cmp_pos[0], cmp_w1[0],
                   cmp_w2[0], out_norm_g[0], w_out[0], ffn_norm_g[0], peer_w_query[0], peer_sub_keys[0],
                   peer_down[0], peer_up[0]) for bi in range(b)]
    return jnp.stack(outs, axis=0)
```
